```python
import math
import jax
import jax.numpy as jnp
from jax import lax
import numpy as np

D_MODEL = 4096
BATCH = 1
SEQ = 16384
DEPTH = 2

GRID_W = 64
CTX_LEN = 256
Q_BLOCK = 128
ROPE_BASE = 10000.0

MLA_HEADS = 8
MLA_NOPE = 128
MLA_ROPE = 64
MLA_V = 128
Q_LORA = 1024
KV_LORA = 512

GQA_Q_HEADS = 16
GQA_KV_HEADS = 4
GQA_HD = 128

CONV_W = 1024
CONV_K = 3

N_BRANCH = 3
N_IN = Q_LORA + KV_LORA + MLA_ROPE + (GQA_Q_HEADS + 2 * GQA_KV_HEADS) * GQA_HD + 3 * CONV_W + N_BRANCH * D_MODEL

N_EXPERTS = 32
TOP_K = 4
D_EXPERT = 512
SWIGLU_LIMIT = 7.0
SWIGLU_ALPHA = 1.702

DEEPNORM_ALPHA = (2 * DEPTH) ** 0.25
DEEPNORM_BETA = (8 * DEPTH) ** -0.25

NORM_EPS = 1e-6

kernel_name = "hybrid_mla_gqa_shortconv_moe_dit"


def _rms(x, g):
    xf = x.astype(jnp.float32)
    y = xf * lax.rsqrt(jnp.mean(xf * xf, axis=-1, keepdims=True) + NORM_EPS)
    return (y * g.astype(jnp.float32)).astype(x.dtype)


def _layernorm(x, g, b):
    xf = x.astype(jnp.float32)
    mu = jnp.mean(xf, axis=-1, keepdims=True)
    var = jnp.mean(jnp.square(xf - mu), axis=-1, keepdims=True)
    y = (xf - mu) * lax.rsqrt(var + NORM_EPS)
    return (y * g.astype(jnp.float32) + b.astype(jnp.float32)).astype(x.dtype)


def _modulate(x, shift, scale):
    return x * (1 + scale) + shift


def _axial_rope_tables(n_ctx, n_lat, rot_dim):
    rows = n_lat // GRID_W
    row = jnp.repeat(jnp.arange(rows, dtype=jnp.float32), GRID_W)
    col = jnp.tile(jnp.arange(GRID_W, dtype=jnp.float32), rows)
    nf = rot_dim // 4
    inv = ROPE_BASE ** (-jnp.arange(nf, dtype=jnp.float32) / nf)
    ang = jnp.concatenate([row[:, None] * inv, col[:, None] * inv], axis=-1)
    ang = jnp.concatenate([jnp.zeros((n_ctx, rot_dim // 2), jnp.float32), ang], axis=0)
    return jnp.cos(ang), jnp.sin(ang)


def _rope(x, cos, sin):
    half = x.shape[-1] // 2
    x1, x2 = x[..., :half], x[..., half:]
    cos = cos.astype(x.dtype)
    sin = sin.astype(x.dtype)
    return jnp.concatenate([x1 * cos - x2 * sin, x2 * cos + x1 * sin], axis=-1)


def _block_attention(q, k, v, scale):
    b, tq, hk, g, dk = q.shape
    nb = tq // Q_BLOCK
    qb = jnp.moveaxis(q.reshape(b, nb, Q_BLOCK, hk, g, dk), 1, 0)

    def one(qi):
        s = jnp.einsum('bqhgd,bkhd->bhgqk', qi, k, preferred_element_type=jnp.float32) * scale
        p = jax.nn.softmax(s, axis=-1).astype(v.dtype)
        return jnp.einsum('bhgqk,bkhd->bqhgd', p, v)

    o = lax.map(one, qb)
    return jnp.moveaxis(o, 0, 1).reshape(b, tq, hk, g, v.shape[-1])


def _prefix_attention(q, k, v, n_ctx, need_ctx, scale):
    lat = _block_attention(q[:, n_ctx:], k, v, scale)
    if need_ctx:
        cto = _block_attention(q[:, :n_ctx], k[:, :n_ctx], v[:, :n_ctx], scale)
        return jnp.concatenate([cto, lat], axis=1)
    return lat


def _dwconv(u, w):
    pad = CONV_K // 2
    return lax.conv_general_dilated(u, w[:, None, :].astype(u.dtype), window_strides=(1,),
                                    padding=[(pad, pad)], dimension_numbers=('NWC', 'WIO', 'NWC'),
                                    feature_group_count=u.shape[-1])


def _split_cols(z):
    sizes = (Q_LORA, KV_LORA, MLA_ROPE, GQA_Q_HEADS * GQA_HD, GQA_KV_HEADS * GQA_HD,
             GQA_KV_HEADS * GQA_HD, CONV_W, CONV_W, CONV_W, N_BRANCH * D_MODEL)
    out = []
    o = 0
    for s in sizes:
        out.append(z[..., o:o + s])
        o += s
    return out


def _token_mixers(h, n_ctx, need_ctx, w_in, g_q_lat, g_kv_lat, w_q_up, w_kv_up, g_q_head, g_k_head,
                  conv_w, w_br_mla, w_br_gqa, w_br_conv, w_out):
    b, t, _ = h.shape
    n_lat = t - n_ctx
    rows = slice(0, None) if need_ctx else slice(n_ctx, None)
    z = h @ w_in
    q_lat, kv_lat, k_pe, gq, gk, gv, cb, cc, cx, gates = _split_cols(z)

    cos_a, sin_a = _axial_rope_tables(n_ctx, n_lat, MLA_ROPE)
    q = (_rms(q_lat, g_q_lat) @ w_q_up).reshape(b, t, MLA_HEADS, MLA_NOPE + MLA_ROPE)
    q = jnp.concatenate([q[..., :MLA_NOPE], _rope(q[..., MLA_NOPE:], cos_a[:, None], sin_a[:, None])], axis=-1)
    kv = (_rms(kv_lat, g_kv_lat) @ w_kv_up).reshape(b, t, MLA_HEADS, MLA_NOPE + MLA_V)
    k_pe = _rope(k_pe, cos_a, sin_a)
    k = jnp.concatenate([kv[..., :MLA_NOPE],
                         jnp.broadcast_to(k_pe[:, :, None, :], (b, t, MLA_HEADS, MLA_ROPE))], axis=-1)
    v = kv[..., MLA_NOPE:]
    mla = _prefix_attention(q[:, :, :, None, :], k, v, n_ctx, need_ctx, (MLA_NOPE + MLA_ROPE) ** -0.5)
    tq = mla.shape[1]
    mla = mla.reshape(b, tq, MLA_HEADS * MLA_V)

    cos_b, sin_b = _axial_rope_tables(n_ctx, n_lat, GQA_HD)
    gq = _rope(_rms(gq.reshape(b, t, GQA_Q_HEADS, GQA_HD), g_q_head), cos_b[:, None], sin_b[:, None])
    gk = _rope(_rms(gk.reshape(b, t, GQA_KV_HEADS, GQA_HD), g_k_head), cos_b[:, None], sin_b[:, None])
    gv = gv.reshape(b, t, GQA_KV_HEADS, GQA_HD)
    gq = gq.reshape(b, t, GQA_KV_HEADS, GQA_Q_HEADS // GQA_KV_HEADS, GQA_HD)
    gqa = _prefix_attention(gq, gk, gv, n_ctx, need_ctx, GQA_HD ** -0.5).reshape(b, tq, GQA_Q_HEADS * GQA_HD)

    u = cc * cx
    y = _dwconv(u[:, n_ctx:], conv_w)
    if need_ctx:
        y = jnp.concatenate([_dwconv(u[:, :n_ctx], conv_w), y], axis=1)
    conv = cb[:, rows] * y

    g = jax.nn.sigmoid(gates[:, rows].astype(jnp.float32)).astype(h.dtype).reshape(b, tq, N_BRANCH, D_MODEL)
    merged = (g[:, :, 0] * (mla @ w_br_mla) + g[:, :, 1] * (gqa @ w_br_gqa)
              + g[:, :, 2] * (conv @ w_br_conv))
    return merged @ w_out


def _moe(h, w_router, b_router, w_gate_up, b_gate_up, w_down, b_down):
    b, t, d = h.shape
    xt = h.reshape(b * t, d)
    n = xt.shape[0]
    logits = (xt @ w_router + b_router).astype(jnp.float32)
    top_val, top_idx = lax.top_k(logits, TOP_K)
    wts = jax.nn.softmax(top_val, axis=-1)
    flat_e = top_idx.reshape(-1)
    order = jnp.argsort(flat_e)
    tok = order // TOP_K
    e_sorted = flat_e[order]
    sizes = jnp.bincount(flat_e, length=N_EXPERTS).astype(jnp.int32)
    xs = xt[tok]
    gu = lax.ragged_dot(xs, w_gate_up, sizes) + b_gate_up[e_sorted]
    gate = jnp.minimum(gu[:, 0::2], SWIGLU_LIMIT)
    up = jnp.clip(gu[:, 1::2], -SWIGLU_LIMIT, SWIGLU_LIMIT)
    act = (up + 1) * (gate * jax.nn.sigmoid(SWIGLU_ALPHA * gate))
    y = lax.ragged_dot(act, w_down, sizes) + b_down[e_sorted]
    y = y * wts.reshape(-1)[order][:, None].astype(y.dtype)
    out = jax.ops.segment_sum(y, tok, num_segments=n)
    return out.reshape(b, t, d)


def setup_inputs(seed: int = 0) -> dict:
    key = jax.random.key(seed)
    keys = iter(jax.random.split(key, 40))
    f32 = jnp.float32

    def nrm(shape, s):
        return jax.random.normal(next(keys), shape, f32) * s

    L, D = DEPTH, D_MODEL
    return {
        "x": nrm((BATCH, SEQ, D), 1.0),
        "c": nrm((BATCH, D), 1.0),
        "ctx": nrm((BATCH, CTX_LEN, D), 1.0),
        "c_ctx": nrm((D,), 1.0),
        "w_ada": nrm((L, D, 6 * D), D ** -0.5),
        "b_ada": nrm((L, 6 * D), 0.01),
        "w_in": nrm((L, D, N_IN), D ** -0.5),
        "g_q_lat": 1.0 + nrm((L, Q_LORA), 0.02),
        "g_kv_lat": 1.0 + nrm((L, KV_LORA), 0.02),
        "w_q_up": nrm((L, Q_LORA, MLA_HEADS * (MLA_NOPE + MLA_ROPE)), Q_LORA ** -0.5),
        "w_kv_up": nrm((L, KV_LORA, MLA_HEADS * (MLA_NOPE + MLA_V)), KV_LORA ** -0.5),
        "g_q_head": 1.0 + nrm((L, GQA_HD), 0.02),
        "g_k_head": 1.0 + nrm((L, GQA_HD), 0.02),
        "conv_w": nrm((L, CONV_K, CONV_W), CONV_K ** -0.5),
        "w_br_mla": nrm((L, MLA_HEADS * MLA_V, D), (MLA_HEADS * MLA_V) ** -0.5),
        "w_br_gqa": nrm((L, GQA_Q_HEADS * GQA_HD, D), (GQA_Q_HEADS * GQA_HD) ** -0.5),
        "w_br_conv": nrm((L, CONV_W, D), CONV_W ** -0.5),
        "w_out": nrm((L, D, D), D ** -0.5 * DEEPNORM_BETA),
        "ln1_g": 1.0 + nrm((L, D), 0.02),
        "ln1_b": nrm((L, D), 0.01),
        "w_router": nrm((L, D, N_EXPERTS), D ** -0.5),
        "b_router": nrm((L, N_EXPERTS), 0.01),
        "w_gate_up": nrm((L, N_EXPERTS, D, 2 * D_EXPERT), D ** -0.5),
        "b_gate_up": nrm((L, N_EXPERTS, 2 * D_EXPERT), 0.01),
        "w_down": nrm((L, N_EXPERTS, D_EXPERT, D), D_EXPERT ** -0.5 * DEEPNORM_BETA),
        "b_down": nrm((L, N_EXPERTS, D), 0.01),
        "ln2_g": 1.0 + nrm((L, D), 0.02),
        "ln2_b": nrm((L, D), 0.01),
    }


def reference(x, c, ctx, c_ctx, w_ada, b_ada, w_in, g_q_lat, g_kv_lat, w_q_up, w_kv_up, g_q_head, g_k_head,
              conv_w, w_br_mla, w_br_gqa, w_br_conv, w_out, ln1_g, ln1_b, w_router, b_router, w_gate_up,
              b_gate_up, w_down, b_down, ln2_g, ln2_b):
    b, s, d = x.shape
    n_ctx = ctx.shape[1]
    xx, xc = x, ctx
    for i in range(DEPTH):
        need_ctx = i < DEPTH - 1
        mod_x = (jax.nn.silu(c) @ w_ada[i] + b_ada[i]).reshape(b, 6, 1, d)
        mod_c = (jax.nn.silu(c_ctx) @ w_ada[i] + b_ada[i]).reshape(6, d)

        h = jnp.concatenate([_modulate(xc, mod_c[0], mod_c[1]),
                             _modulate(xx, mod_x[:, 0], mod_x[:, 1])], axis=1)
        a = _token_mixers(h, n_ctx, need_ctx, w_in[i], g_q_lat[i], g_kv_lat[i], w_q_up[i], w_kv_up[i],
                          g_q_head[i], g_k_head[i], conv_w[i], w_br_mla[i], w_br_gqa[i], w_br_conv[i],
                          w_out[i])
        xx = _layernorm(DEEPNORM_ALPHA * xx + mod_x[:, 2] * a[:, -s:], ln1_g[i], ln1_b[i])
        if need_ctx:
            xc = _layernorm(DEEPNORM_ALPHA * xc + mod_c[2] * a[:, :n_ctx], ln1_g[i], ln1_b[i])

        hx = _modulate(xx, mod_x[:, 3], mod_x[:, 4])
        if need_ctx:
            f = _moe(jnp.concatenate([_modulate(xc, mod_c[3], mod_c[4]), hx], axis=1), w_router[i],
                     b_router[i], w_gate_up[i], b_gate_up[i], w_down[i], b_down[i])
            xc = _layernorm(DEEPNORM_ALPHA * xc + mod_c[5] * f[:, :n_ctx], ln2_g[i], ln2_b[i])
            fx = f[:, n_ctx:]
        else:
            fx = _moe(hx, w_router[i], b_router[i], w_gate_up[i], b_gate_up[i], w_down[i], b_down[i])
        xx = _layernorm(DEEPNORM_ALPHA * xx + mod_x[:, 5] * fx, ln2_g[i], ln2_b[i])
    return xx
```

```python
import functools
import math

import jax
import jax.numpy as jnp
from jax import lax
from jax.experimental import pallas as pl
from jax.experimental.pallas import tpu as pltpu

GRID_W = 64
ROPE_BASE = 10000.0
MLA_HEADS = 8
MLA_NOPE = 128
MLA_ROPE = 64
MLA_V = 128
GQA_Q_HEADS = 16
GQA_KV_HEADS = 4
GQA_HD = 128
CONV_K = 3
N_BRANCH = 3
TOP_K = 4
SWIGLU_LIMIT = 7.0
SWIGLU_ALPHA = 1.702
NORM_EPS = 1e-6

LANE = 128
SUBLANE = 8
VMEM_LIMIT = 56 * 1024 * 1024
ROW_TILE = 256
MOE_TILE = 256
COMBINE_TILE = 64
LOG2E = 1.4426950408889634
NEG_BIG = -1e30

F32 = jnp.float32
BF16 = jnp.bfloat16


def _cparams(sem, vmem=VMEM_LIMIT, **kw):
    return pltpu.CompilerParams(dimension_semantics=sem, vmem_limit_bytes=vmem, **kw)


def _div_tile(n, cap, quantum):
    best = None
    t = quantum
    while t <= min(n, cap):
        if n % t == 0:
            best = t
        t += quantum
    assert best is not None, (n, cap, quantum)
    return best


def _ada_kernel(c_ref, w_ref, b_ref, o_ref):
    c = c_ref[...]
    s = (c * jax.nn.sigmoid(c)).astype(BF16)
    o_ref[0] = jnp.dot(s, w_ref[0].astype(BF16), preferred_element_type=F32) + b_ref[0]


def _ada(cc, w_ada, b_ada):
    nl, d, n = w_ada.shape
    tn = _div_tile(n, 512, LANE)
    return pl.pallas_call(
        _ada_kernel,
        out_shape=jax.ShapeDtypeStruct((nl, SUBLANE, n), F32),
        grid=(nl, n // tn),
        in_specs=[
            pl.BlockSpec((SUBLANE, d), lambda l, j: (0, 0)),
            pl.BlockSpec((1, d, tn), lambda l, j: (l, 0, j)),
            pl.BlockSpec((1, 1, tn), lambda l, j: (l, 0, j)),
        ],
        out_specs=pl.BlockSpec((1, SUBLANE, tn), lambda l, j: (l, 0, j)),
        compiler_params=_cparams(("parallel", "parallel")),
        name="ada_mod",
    )(cc, w_ada, b_ada.reshape(nl, 1, n))


def _mm_kernel(*refs, n_rows, n_cols, epilogue):
    a_ref, b_ref = refs[0], refs[1]
    row_refs = refs[2:2 + n_rows]
    col_refs = refs[2 + n_rows:2 + n_rows + n_cols]
    out_refs = refs[2 + n_rows + n_cols:]
    acc = jnp.dot(a_ref[...], b_ref[...], preferred_element_type=F32)
    epilogue(acc, row_refs, col_refs, out_refs)


def _matmul(a, b, *, tm, tn, outs, epilogue, rows=(), cols=(), name):
    m, k = a.shape
    n = b.shape[1]
    nj = n // tn
    in_specs = [pl.BlockSpec((tm, k), lambda i, j: (i, 0)), pl.BlockSpec((k, tn), lambda i, j: (0, j))]
    for r in rows:
        in_specs.append(pl.BlockSpec((tm, r.shape[1]), lambda i, j: (i, 0)))
    for c in cols:
        in_specs.append(pl.BlockSpec((c.shape[0], tn), lambda i, j: (0, j)))
    out_shape = [jax.ShapeDtypeStruct((m, w), dt) for (w, dt) in outs]
    out_specs = [pl.BlockSpec((tm, w // nj), lambda i, j: (i, j)) for (w, dt) in outs]
    res = pl.pallas_call(
        functools.partial(_mm_kernel, n_rows=len(rows), n_cols=len(cols), epilogue=epilogue),
        out_shape=out_shape,
        grid=(m // tm, nj),
        in_specs=in_specs,
        out_specs=out_specs,
        compiler_params=_cparams(("parallel", "parallel")),
        name=name,
    )(a, b, *rows, *cols)
    return res


def _rms_rows(x, g):
    return x * lax.rsqrt(jnp.mean(x * x, axis=1, keepdims=True) + NORM_EPS) * g


def _rope_block(x, cos, sin_signed):
    return x * cos + pltpu.roll(x, LANE // 2, axis=1) * sin_signed


def _ep_store(acc, row_refs, col_refs, out_refs):
    out_refs[0][...] = acc.astype(out_refs[0].dtype)


def _ep_sigmoid(acc, row_refs, col_refs, out_refs):
    out_refs[0][...] = jax.nn.sigmoid(acc).astype(out_refs[0].dtype)


def _ep_rms(acc, row_refs, col_refs, out_refs):
    out_refs[0][...] = _rms_rows(acc, col_refs[0][...]).astype(out_refs[0].dtype)


def _ep_kvpe(acc, row_refs, col_refs, out_refs, *, kvl):
    out_refs[0][...] = _rms_rows(acc[:, :kvl], col_refs[0][:, :kvl]).astype(out_refs[0].dtype)
    out_refs[1][...] = _rope_block(acc[:, kvl:], row_refs[0][...], row_refs[1][...]).astype(out_refs[1].dtype)


def _ep_gqk(acc, row_refs, col_refs, out_refs):
    cos = row_refs[0][...]
    sin = row_refs[1][...]
    gain = col_refs[0][...]
    scale = col_refs[1][...]
    for h in range(acc.shape[1] // GQA_HD):
        sl = slice(h * GQA_HD, (h + 1) * GQA_HD)
        y = _rms_rows(acc[:, sl], gain[:, sl])
        out_refs[0][:, sl] = (_rope_block(y, cos, sin) * scale[:, sl]).astype(out_refs[0].dtype)


def _ep_qup(acc, row_refs, col_refs, out_refs, *, qscale):
    cos = row_refs[0][...]
    sin = row_refs[1][...]
    for h in range(MLA_HEADS):
        a0 = h * 2 * LANE
        out_refs[0][:, a0:a0 + LANE] = (acc[:, a0:a0 + LANE] * qscale).astype(out_refs[0].dtype)
        r = _rope_block(acc[:, a0 + LANE:a0 + 2 * LANE], cos, sin) * qscale
        out_refs[0][:, a0 + LANE:a0 + 2 * LANE] = r.astype(out_refs[0].dtype)


def _ep_kvup(acc, row_refs, col_refs, out_refs):
    kpe = row_refs[0][...]
    hw = MLA_HEADS * MLA_NOPE
    for h in range(MLA_HEADS):
        out_refs[0][:, h * 2 * LANE:h * 2 * LANE + LANE] = acc[:, h * LANE:(h + 1) * LANE].astype(out_refs[0].dtype)
        out_refs[0][:, h * 2 * LANE + LANE:(h + 1) * 2 * LANE] = kpe
    out_refs[1][...] = acc[:, hw:].astype(out_refs[1].dtype)


def _mod_index(n_ctx_tiles):
    return lambda i: (jnp.where(i < n_ctx_tiles, 1, 0), 0, 0)


def _modulate_kernel(x_ref, mod_ref, h_ref):
    m = mod_ref[0]
    h_ref[...] = (x_ref[...] * (1.0 + m[1:2, :]) + m[0:1, :]).astype(h_ref.dtype)


def _modulate(xa, mods, n_ctx):
    t, d = xa.shape
    return pl.pallas_call(
        _modulate_kernel,
        out_shape=jax.ShapeDtypeStruct((t, d), BF16),
        grid=(t // ROW_TILE,),
        in_specs=[pl.BlockSpec((ROW_TILE, d), lambda i: (i, 0)),
                  pl.BlockSpec((1, 6, d), _mod_index(n_ctx // ROW_TILE))],
        out_specs=pl.BlockSpec((ROW_TILE, d), lambda i: (i, 0)),
        compiler_params=_cparams(("parallel",)),
        name="modulate",
    )(xa, mods)


def _layernorm_rows(v, g, b):
    mu = jnp.mean(v, axis=1, keepdims=True)
    vc = v - mu
    var = jnp.mean(vc * vc, axis=1, keepdims=True)
    return vc * lax.rsqrt(var + NORM_EPS) * g + b


def _ln_res_kernel(x_ref, a_ref, mod_ref, g_ref, b_ref, y_ref, h_ref, *, alpha):
    m = mod_ref[0]
    y = _layernorm_rows(alpha * x_ref[...] + m[2:3, :] * a_ref[...], g_ref[...], b_ref[...])
    y_ref[...] = y
    h_ref[...] = y * (1.0 + m[4:5, :]) + m[3:4, :]


def _ln_res(xa, a, mods, g, b, n_ctx, alpha):
    t, d = xa.shape
    row = pl.BlockSpec((ROW_TILE, d), lambda i: (i, 0))
    vec = pl.BlockSpec((1, d), lambda i: (0, 0))
    return pl.pallas_call(
        functools.partial(_ln_res_kernel, alpha=alpha),
        out_shape=[jax.ShapeDtypeStruct((t, d), F32), jax.ShapeDtypeStruct((t, d), F32)],
        grid=(t // ROW_TILE,),
        in_specs=[row, row, pl.BlockSpec((1, 6, d), _mod_index(n_ctx // ROW_TILE)), vec, vec],
        out_specs=[row, row],
        compiler_params=_cparams(("parallel",)),
        name="ln_res",
    )(xa, a, mods, g.reshape(1, d), b.reshape(1, d))


def _conv_kernel(z_ref, zp_ref, zn_ref, w_ref, o_ref, *, cw, ctx_tiles, n_tiles):
    i = pl.program_id(0)
    starts = jnp.logical_or(i == 0, i == ctx_tiles)
    ends = jnp.logical_or(i == ctx_tiles - 1, i == n_tiles - 1)
    cb = z_ref[:, 0:cw]
    u = z_ref[:, cw:2 * cw] * z_ref[:, 2 * cw:3 * cw]
    up = zp_ref[SUBLANE - 1:SUBLANE, cw:2 * cw] * zp_ref[SUBLANE - 1:SUBLANE, 2 * cw:3 * cw]
    un = zn_ref[0:1, cw:2 * cw] * zn_ref[0:1, 2 * cw:3 * cw]
    up = jnp.where(starts, 0.0, up)
    un = jnp.where(ends, 0.0, un)
    rows = u.shape[0]
    rid = lax.broadcasted_iota(jnp.int32, u.shape, 0)
    u_prev = jnp.where(rid == 0, up, pltpu.roll(u, 1, axis=0))
    u_next = jnp.where(rid == rows - 1, un, pltpu.roll(u, rows - 1, axis=0))
    y = w_ref[0:1, :] * u_prev + w_ref[1:2, :] * u + w_ref[2:3, :] * u_next
    o_ref[...] = (cb * y).astype(o_ref.dtype)


def _short_conv(zc, conv_w, n_ctx):
    t, w3 = zc.shape
    cw = w3 // 3
    n_tiles = t // ROW_TILE
    per = ROW_TILE // SUBLANE
    last8 = t // SUBLANE - 1
    return pl.pallas_call(
        functools.partial(_conv_kernel, cw=cw, ctx_tiles=n_ctx // ROW_TILE, n_tiles=n_tiles),
        out_shape=jax.ShapeDtypeStruct((t, cw), BF16),
        grid=(n_tiles,),
        in_specs=[
            pl.BlockSpec((ROW_TILE, w3), lambda i: (i, 0)),
            pl.BlockSpec((SUBLANE, w3), lambda i: (jnp.maximum(i * per - 1, 0), 0)),
            pl.BlockSpec((SUBLANE, w3), lambda i: (jnp.minimum((i + 1) * per, last8), 0)),
            pl.BlockSpec((CONV_K, cw), lambda i: (0, 0)),
        ],
        out_specs=pl.BlockSpec((ROW_TILE, cw), lambda i: (i, 0)),
        compiler_params=_cparams(("parallel",)),
        name="short_conv",
    )(zc, zc, zc, conv_w)


def _router_kernel(h_ref, w_ref, b_ref, idx_ref, wt_ref):
    logits = jnp.dot(h_ref[...].astype(BF16), w_ref[...], preferred_element_type=F32) + b_ref[...]
    lane = lax.broadcasted_iota(jnp.int32, logits.shape, 1)
    vals, idxs = [], []
    cur = logits
    for _ in range(TOP_K):
        mx = jnp.max(cur, axis=1, keepdims=True)
        ix = jnp.min(jnp.where(cur == mx, lane, LANE), axis=1, keepdims=True)
        vals.append(mx)
        idxs.append(ix)
        cur = jnp.where(lane == ix, -jnp.inf, cur)
    es = [jnp.exp(v - vals[0]) for v in vals]
    tot = es[0]
    for e in es[1:]:
        tot = tot + e
    idx_out = jnp.zeros(logits.shape, jnp.int32)
    wt_out = jnp.zeros(logits.shape, F32)
    for k in range(TOP_K):
        idx_out = jnp.where(lane == k, idxs[k], idx_out)
        wt_out = jnp.where(lane == k, es[k] / tot, wt_out)
    idx_ref[...] = idx_out
    wt_ref[...] = wt_out


def _router(hx, w_router_p, b_router_p):
    t, d = hx.shape
    out = pl.BlockSpec((ROW_TILE, LANE), lambda i: (i, 0))
    return pl.pallas_call(
        _router_kernel,
        out_shape=[jax.ShapeDtypeStruct((t, LANE), jnp.int32), jax.ShapeDtypeStruct((t, LANE), F32)],
        grid=(t // ROW_TILE,),
        in_specs=[pl.BlockSpec((ROW_TILE, d), lambda i: (i, 0)),
                  pl.BlockSpec((d, LANE), lambda i: (0, 0)),
                  pl.BlockSpec((1, LANE), lambda i: (0, 0))],
        out_specs=[out, out],
        compiler_params=_cparams(("parallel",)),
        name="router",
    )(hx, w_router_p, b_router_p)


def _attn_kernel(q_ref, k_ref, v_ref, o_ref, m_scr, l_scr, acc_scr, *, groups, dk, dv, tk, n_ctx,
                 n_lat_chunks, ctx_tiles):
    i = pl.program_id(1)
    tq = q_ref.shape[0]
    if groups > 1:
        q = jnp.concatenate([q_ref[:, g * dk:(g + 1) * dk] for g in range(groups)], axis=0)
    else:
        q = q_ref[...]
    nt_dims = (((1,), (1,)), ((), ()))

    s = lax.dot_general(q, k_ref[0:n_ctx, :], nt_dims, preferred_element_type=F32)
    m0 = jnp.max(s, axis=1, keepdims=True)
    p = jnp.exp2(s - m0)
    m_scr[...] = m0
    l_scr[...] = jnp.sum(p, axis=1, keepdims=True)
    acc_scr[...] = jnp.dot(p.astype(BF16), v_ref[0:n_ctx, :], preferred_element_type=F32)

    def body(c, carry):
        off = pl.multiple_of(n_ctx + c * tk, LANE)
        kc = k_ref[pl.ds(off, tk), :]
        vc = v_ref[pl.ds(off, tk), :]
        sc = lax.dot_general(q, kc, nt_dims, preferred_element_type=F32)
        m_prev = m_scr[...]
        m_new = jnp.maximum(m_prev, jnp.max(sc, axis=1, keepdims=True))
        pc = jnp.exp2(sc - m_new)
        alpha = jnp.exp2(m_prev - m_new)
        l_scr[...] = alpha * l_scr[...] + jnp.sum(pc, axis=1, keepdims=True)
        acc_scr[...] = alpha * acc_scr[...] + jnp.dot(pc.astype(BF16), vc, preferred_element_type=F32)
        m_scr[...] = m_new
        return carry

    lax.fori_loop(0, jnp.where(i >= ctx_tiles, n_lat_chunks, 0), body, 0)

    o = acc_scr[...] / l_scr[...]
    for g in range(groups):
        o_ref[:, g * dv:(g + 1) * dv] = o[g * tq:(g + 1) * tq].astype(o_ref.dtype)


def _attention(q, k, v, *, n_kv_heads, groups, dk, dv, tq, tk, n_ctx):
    t = q.shape[0]
    n_lat = t - n_ctx
    rows = groups * tq
    return pl.pallas_call(
        functools.partial(_attn_kernel, groups=groups, dk=dk, dv=dv, tk=tk, n_ctx=n_ctx,
                          n_lat_chunks=n_lat // tk, ctx_tiles=n_ctx // tq),
        out_shape=jax.ShapeDtypeStruct((t, n_kv_heads * groups * dv), BF16),
        grid=(n_kv_heads, t // tq),
        in_specs=[
            pl.BlockSpec((tq, groups * dk), lambda h, i: (i, h)),
            pl.BlockSpec((t, dk), lambda h, i: (0, h)),
            pl.BlockSpec((t, dv), lambda h, i: (0, h)),
        ],
        out_specs=pl.BlockSpec((tq, groups * dv), lambda h, i: (i, h)),
        scratch_shapes=[pltpu.VMEM((rows, 1), F32), pltpu.VMEM((rows, 1), F32), pltpu.VMEM((rows, dv), F32)],
        compiler_params=_cparams(("parallel", "parallel")),
        name="prefix_attention",
    )(q, k, v)


def _branch_kernel(m_ref, g_ref, c_ref, wm_ref, wg_ref, wc_ref, s0_ref, s1_ref, s2_ref, o_ref):
    acc = s0_ref[...] * jnp.dot(m_ref[...], wm_ref[...], preferred_element_type=F32)
    acc = acc + s1_ref[...] * jnp.dot(g_ref[...], wg_ref[...], preferred_element_type=F32)
    acc = acc + s2_ref[...] * jnp.dot(c_ref[...], wc_ref[...], preferred_element_type=F32)
    o_ref[...] = acc.astype(o_ref.dtype)


def _branch_merge(mla, gqa, conv, wm, wg, wc, sig, *, tm, tn):
    t = mla.shape[0]
    d = wm.shape[1]
    nj = d // tn

    def rowspec(a):
        return pl.BlockSpec((tm, a.shape[1]), lambda i, j: (i, 0))

    def wspec(w):
        return pl.BlockSpec((w.shape[0], tn), lambda i, j: (0, j))

    def sigspec(b):
        return pl.BlockSpec((tm, tn), lambda i, j: (i, j + b * nj))

    return pl.pallas_call(
        _branch_kernel,
        out_shape=jax.ShapeDtypeStruct((t, d), BF16),
        grid=(t // tm, nj),
        in_specs=[rowspec(mla), rowspec(gqa), rowspec(conv), wspec(wm), wspec(wg), wspec(wc),
                  sigspec(0), sigspec(1), sigspec(2)],
        out_specs=pl.BlockSpec((tm, tn), lambda i, j: (i, j)),
        compiler_params=_cparams(("parallel", "parallel")),
        name="branch_merge",
    )(mla, gqa, conv, wm, wg, wc, sig, sig, sig)


def _moe_ffn_kernel(te_ref, nt_ref, rt_ref, h_hbm, wgu_ref, bgu_ref, wd_ref, bd_ref, y_ref, buf, sem, *, tm, de):
    t = pl.program_id(0)
    n_used = nt_ref[0]

    def row_copy(tile, slot, r):
        tok = rt_ref[tile * tm + r]
        return pltpu.make_async_copy(h_hbm.at[pl.ds(tok, 1), :], buf.at[slot, pl.ds(r, 1), :], sem.at[slot])

    def start_gather(tile, slot):
        def body(r, carry):
            row_copy(tile, slot, r).start()
            return carry
        lax.fori_loop(0, tm, body, 0)

    def wait_gather(tile, slot):
        def body(r, carry):
            row_copy(tile, slot, r).wait()
            return carry
        lax.fori_loop(0, tm, body, 0)

    @pl.when(t == 0)
    def _():
        start_gather(0, 0)

    @pl.when(t + 1 < n_used)
    def _():
        start_gather(t + 1, (t + 1) % 2)

    @pl.when(t < n_used)
    def _():
        slot = t % 2
        wait_gather(t, slot)
        xs = buf[slot].astype(BF16)
        gu = jnp.dot(xs, wgu_ref[0], preferred_element_type=F32) + bgu_ref[0]
        gate = jnp.minimum(gu[:, :de], SWIGLU_LIMIT)
        up = jnp.clip(gu[:, de:], -SWIGLU_LIMIT, SWIGLU_LIMIT)
        act = (up + 1.0) * (gate * jax.nn.sigmoid(SWIGLU_ALPHA * gate))
        y_ref[...] = jnp.dot(act.astype(BF16), wd_ref[0], preferred_element_type=F32) + bd_ref[0]

    @pl.when(t >= n_used)
    def _():
        y_ref[...] = jnp.zeros_like(y_ref)


def _moe_ffn(hx, tile_expert, n_used, row_token, wgu, bgu, wd, bd):
    t, d = hx.shape
    n_exp, _, de2 = wgu.shape
    de = de2 // 2
    n_tiles = tile_expert.shape[0]
    tm = MOE_TILE
    grid_spec = pltpu.PrefetchScalarGridSpec(
        num_scalar_prefetch=3,
        grid=(n_tiles,),
        in_specs=[
            pl.BlockSpec(memory_space=pl.ANY),
            pl.BlockSpec((1, d, de2), lambda i, te, nt, rt: (te[i], 0, 0)),
            pl.BlockSpec((1, 1, de2), lambda i, te, nt, rt: (te[i], 0, 0)),
            pl.BlockSpec((1, de, d), lambda i, te, nt, rt: (te[i], 0, 0)),
            pl.BlockSpec((1, 1, d), lambda i, te, nt, rt: (te[i], 0, 0)),
        ],
        out_specs=pl.BlockSpec((tm, d), lambda i, te, nt, rt: (i, 0)),
        scratch_shapes=[pltpu.VMEM((2, tm, d), F32), pltpu.SemaphoreType.DMA((2,))],
    )
    return pl.pallas_call(
        functools.partial(_moe_ffn_kernel, tm=tm, de=de),
        out_shape=jax.ShapeDtypeStruct((n_tiles * tm, d), F32),
        grid_spec=grid_spec,
        compiler_params=_cparams(("arbitrary",)),
        name="moe_ffn",
    )(tile_expert, n_used, row_token, hx, wgu, bgu.reshape(n_exp, 1, de2), wd, bd.reshape(n_exp, 1, d))


def _combine_kernel(pos_ref, y_hbm, wt_ref, x_ref, mod_ref, g_ref, b_ref, *rest, tt, alpha, with_next):
    if with_next:
        nmod_ref, o_ref, h_ref, buf, sem = rest
    else:
        o_ref, buf, sem = rest
    i = pl.program_id(0)
    n = pl.num_programs(0)

    def row_copy(tile, slot, r, k):
        src = pos_ref[(tile * tt + r) * TOP_K + k]
        return pltpu.make_async_copy(y_hbm.at[pl.ds(src, 1), :], buf.at[slot, k, pl.ds(r, 1), :], sem.at[slot])

    def start_gather(tile, slot):
        def body(r, carry):
            for k in range(TOP_K):
                row_copy(tile, slot, r, k).start()
            return carry
        lax.fori_loop(0, tt, body, 0)

    def wait_gather(tile, slot):
        def body(r, carry):
            for k in range(TOP_K):
                row_copy(tile, slot, r, k).wait()
            return carry
        lax.fori_loop(0, tt, body, 0)

    @pl.when(i == 0)
    def _():
        start_gather(0, 0)

    @pl.when(i + 1 < n)
    def _():
        start_gather(i + 1, (i + 1) % 2)

    slot = i % 2
    wait_gather(i, slot)
    wt = wt_ref[...]
    f = wt[:, 0:1] * buf[slot, 0]
    for k in range(1, TOP_K):
        f = f + wt[:, k:k + 1] * buf[slot, k]
    m = mod_ref[0]
    y = _layernorm_rows(alpha * x_ref[...] + m[5:6, :] * f, g_ref[...], b_ref[...])
    o_ref[...] = y
    if with_next:
        nm = nmod_ref[0]
        h_ref[...] = (y * (1.0 + nm[1:2, :]) + nm[0:1, :]).astype(h_ref.dtype)


def _moe_combine(pos, y_sorted, wts, xa, mods, g, b, next_mods, n_ctx, alpha):
    t, d = xa.shape
    tt = COMBINE_TILE
    with_next = next_mods is not None
    ctx_tiles = n_ctx // tt
    modspec = pl.BlockSpec((1, 6, d), lambda i, pos: (jnp.where(i < ctx_tiles, 1, 0), 0, 0))
    row = pl.BlockSpec((tt, d), lambda i, pos: (i, 0))
    vec = pl.BlockSpec((1, d), lambda i, pos: (0, 0))
    in_specs = [pl.BlockSpec(memory_space=pl.ANY), pl.BlockSpec((tt, LANE), lambda i, pos: (i, 0)), row, modspec,
                vec, vec]
    args = [pos, y_sorted, wts, xa, mods, g.reshape(1, d), b.reshape(1, d)]
    out_shape = [jax.ShapeDtypeStruct((t, d), F32)]
    out_specs = [row]
    if with_next:
        in_specs.append(modspec)
        args.append(next_mods)
        out_shape.append(jax.ShapeDtypeStruct((t, d), BF16))
        out_specs.append(row)
    grid_spec = pltpu.PrefetchScalarGridSpec(
        num_scalar_prefetch=1,
        grid=(t // tt,),
        in_specs=in_specs,
        out_specs=out_specs,
        scratch_shapes=[pltpu.VMEM((2, TOP_K, tt, d), F32), pltpu.SemaphoreType.DMA((2,))],
    )
    res = pl.pallas_call(
        functools.partial(_combine_kernel, tt=tt, alpha=alpha, with_next=with_next),
        out_shape=out_shape,
        grid_spec=grid_spec,
        compiler_params=_cparams(("arbitrary",)),
        name="moe_combine_ln",
    )(*args)
    return res if with_next else (res[0], None)


def _routing_tables(top_idx, n_exp, tm):
    t = top_idx.shape[0]
    n_pairs = t * TOP_K
    n_tiles = -(-(n_pairs + n_exp * (tm - 1)) // tm)
    flat_e = top_idx.reshape(-1)
    order = jnp.argsort(flat_e, stable=True).astype(jnp.int32)
    e_sorted = flat_e[order]
    sizes = jnp.bincount(flat_e, length=n_exp).astype(jnp.int32)
    padded = ((sizes + tm - 1) // tm) * tm
    pend = jnp.cumsum(padded)
    pstart = pend - padded
    gstart = jnp.cumsum(sizes) - sizes
    dest = pstart[e_sorted] + (jnp.arange(n_pairs, dtype=jnp.int32) - gstart[e_sorted])
    row_token = jnp.zeros((n_tiles * tm,), jnp.int32).at[dest].set(order // TOP_K)
    pos = jnp.zeros((n_pairs,), jnp.int32).at[order].set(dest)
    tile_start = jnp.arange(n_tiles, dtype=jnp.int32) * tm
    tile_expert = jnp.minimum(jnp.searchsorted(pend, tile_start, side="right"), n_exp - 1).astype(jnp.int32)
    n_used = (pend[-1] // tm).astype(jnp.int32).reshape(1)
    return tile_expert, n_used, row_token, pos


def _rope_tables(n_ctx, n_lat, rot_dim):
    rows = n_lat // GRID_W
    row = jnp.repeat(jnp.arange(rows, dtype=F32), GRID_W)
    col = jnp.tile(jnp.arange(GRID_W, dtype=F32), rows)
    nf = rot_dim // 4
    inv = ROPE_BASE ** (-jnp.arange(nf, dtype=F32) / nf)
    ang = jnp.concatenate([row[:, None] * inv, col[:, None] * inv], axis=-1)
    ang = jnp.concatenate([jnp.zeros((n_ctx, rot_dim // 2), F32), ang], axis=0)
    return jnp.cos(ang), jnp.sin(ang)


def _pad_half(a, width):
    return jnp.pad(a, ((0, 0), (0, width - a.shape[1])))


def kernel(x, c, ctx, c_ctx, w_ada, b_ada, w_in, g_q_lat, g_kv_lat, w_q_up, w_kv_up, g_q_head, g_k_head, conv_w,
           w_br_mla, w_br_gqa, w_br_conv, w_out, ln1_g, ln1_b, w_router, b_router, w_gate_up, b_gate_up, w_down,
           b_down, ln2_g, ln2_b):
    bsz, s, d = x.shape
    assert bsz == 1, "single-sample prefill"
    n_ctx = ctx.shape[1]
    t = n_ctx + s
    depth = w_ada.shape[0]
    ql = g_q_lat.shape[1]
    kvl = g_kv_lat.shape[1]
    cw = conv_w.shape[2]
    n_exp = w_router.shape[2]
    de = w_down.shape[2]
    alpha = (2 * depth) ** 0.25
    assert n_ctx % ROW_TILE == 0 and s % ROW_TILE == 0 and n_exp <= LANE
    half = MLA_ROPE // 2
    quarter = LANE // 4

    cos_a, sin_a = _rope_tables(n_ctx, s, MLA_ROPE)
    cos_a_p = jnp.concatenate([_pad_half(cos_a, LANE // 2), _pad_half(cos_a, LANE // 2)], axis=1)
    sin_a_p = jnp.concatenate([_pad_half(-sin_a, LANE // 2), _pad_half(sin_a, LANE // 2)], axis=1)
    cos_b, sin_b = _rope_tables(n_ctx, s, GQA_HD)
    cos_b_p = jnp.concatenate([cos_b, cos_b], axis=1)
    sin_b_p = jnp.concatenate([-sin_b, sin_b], axis=1)
    del quarter

    cc = jnp.zeros((SUBLANE, d), F32).at[0].set(c[0]).at[1].set(c_ctx)
    mods_all = _ada(cc, w_ada, b_ada).reshape(depth, SUBLANE, 6, d)[:, :2]

    xa = jnp.concatenate([ctx[0], x[0]], axis=0)
    h = _modulate(xa, mods_all[0], n_ctx)

    tm_big = _div_tile(t, 1280, ROW_TILE)
    tm_mid = _div_tile(t, 640, SUBLANE * 16)
    q_scale_mla = (MLA_NOPE + MLA_ROPE) ** -0.5 * LOG2E
    q_scale_gqa = GQA_HD ** -0.5 * LOG2E

    for li in range(depth):
        mods = mods_all[li]
        wi = w_in[li]
        o = 0
        w_qlat = wi[:, o:o + ql].astype(BF16); o += ql
        w_kvlat = wi[:, o:o + kvl]; o += kvl
        w_kpe = wi[:, o:o + MLA_ROPE]; o += MLA_ROPE
        w_kpe_p = jnp.concatenate([_pad_half(w_kpe[:, :half], LANE // 2), _pad_half(w_kpe[:, half:], LANE // 2)], 1)
        w_kvpe = jnp.concatenate([w_kvlat, w_kpe_p], axis=1).astype(BF16)
        n_gqk = (GQA_Q_HEADS + GQA_KV_HEADS) * GQA_HD
        w_gqk = wi[:, o:o + n_gqk].astype(BF16); o += n_gqk
        n_gv = GQA_KV_HEADS * GQA_HD
        w_gv = wi[:, o:o + n_gv].astype(BF16); o += n_gv
        w_cv = wi[:, o:o + 3 * cw].astype(BF16); o += 3 * cw
        w_gt = wi[:, o:o + N_BRANCH * d].astype(BF16); o += N_BRANCH * d
        assert o == wi.shape[1]

        wq = w_q_up[li].reshape(ql, MLA_HEADS, MLA_NOPE + MLA_ROPE)
        zq = jnp.zeros((ql, MLA_HEADS, LANE // 2 - half), wq.dtype)
        wq_p = jnp.concatenate([wq[:, :, :MLA_NOPE], wq[:, :, MLA_NOPE:MLA_NOPE + half], zq,
                                wq[:, :, MLA_NOPE + half:], zq], axis=2).reshape(ql, MLA_HEADS * 2 * LANE)
        wq_p = wq_p.astype(BF16)
        wkv = w_kv_up[li].reshape(kvl, MLA_HEADS, MLA_NOPE + MLA_V)
        wkv_p = jnp.concatenate([wkv[:, :, :MLA_NOPE].reshape(kvl, -1), wkv[:, :, MLA_NOPE:].reshape(kvl, -1)],
                                axis=1).astype(BF16)

        gain_qk = jnp.concatenate([jnp.tile(g_q_head[li], GQA_Q_HEADS), jnp.tile(g_k_head[li], GQA_KV_HEADS)])
        scale_qk = jnp.concatenate([jnp.full((GQA_Q_HEADS * GQA_HD,), q_scale_gqa, F32),
                                    jnp.ones((GQA_KV_HEADS * GQA_HD,), F32)])
        g_kvpe = jnp.concatenate([g_kv_lat[li], jnp.ones((LANE,), F32)]).reshape(1, kvl + LANE)

        (qn,) = _matmul(h, w_qlat, tm=tm_mid, tn=ql, outs=[(ql, BF16)], epilogue=_ep_rms,
                        cols=[g_q_lat[li].reshape(1, ql)], name="in_qlat")
        kvn, kpe = _matmul(h, w_kvpe, tm=tm_mid, tn=kvl + LANE, outs=[(kvl, BF16), (LANE, BF16)],
                           epilogue=functools.partial(_ep_kvpe, kvl=kvl), rows=[cos_a_p, sin_a_p], cols=[g_kvpe],
                           name="in_kvpe")
        (gqk,) = _matmul(h, w_gqk, tm=tm_big, tn=4 * GQA_HD, outs=[(n_gqk, BF16)], epilogue=_ep_gqk,
                         rows=[cos_b_p, sin_b_p], cols=[gain_qk.reshape(1, -1), scale_qk.reshape(1, -1)],
                         name="in_gqk")
        (gv,) = _matmul(h, w_gv, tm=tm_big, tn=n_gv, outs=[(n_gv, BF16)], epilogue=_ep_store, name="in_gv")
        (zc,) = _matmul(h, w_cv, tm=tm_big, tn=_div_tile(3 * cw, 512, LANE), outs=[(3 * cw, F32)],
                        epilogue=_ep_store, name="in_conv")
        (sig,) = _matmul(h, w_gt, tm=tm_big, tn=_div_tile(N_BRANCH * d, 512, LANE), outs=[(N_BRANCH * d, F32)],
                         epilogue=_ep_sigmoid, name="in_gates")

        (q_mla,) = _matmul(qn, wq_p, tm=tm_big, tn=MLA_HEADS * 2 * LANE, outs=[(MLA_HEADS * 2 * LANE, BF16)],
                           epilogue=functools.partial(_ep_qup, qscale=q_scale_mla), rows=[cos_a_p, sin_a_p],
                           name="mla_q_up")
        k_mla, v_mla = _matmul(kvn, wkv_p, tm=tm_big, tn=MLA_HEADS * 2 * LANE,
                               outs=[(MLA_HEADS * 2 * LANE, BF16), (MLA_HEADS * MLA_V, BF16)], epilogue=_ep_kvup,
                               rows=[kpe], name="mla_kv_up")
        tk = _div_tile(s, 1024, ROW_TILE)
        mla = _attention(q_mla, k_mla, v_mla, n_kv_heads=MLA_HEADS, groups=1, dk=2 * LANE, dv=MLA_V, tq=ROW_TILE,
                         tk=tk, n_ctx=n_ctx)

        nq = GQA_Q_HEADS * GQA_HD
        gqa = _attention(gqk[:, :nq], gqk[:, nq:], gv, n_kv_heads=GQA_KV_HEADS,
                         groups=GQA_Q_HEADS // GQA_KV_HEADS, dk=GQA_HD, dv=GQA_HD, tq=ROW_TILE // 2, tk=tk,
                         n_ctx=n_ctx)

        conv = _short_conv(zc, conv_w[li], n_ctx)

        merged = _branch_merge(mla, gqa, conv, w_br_mla[li].astype(BF16), w_br_gqa[li].astype(BF16),
                               w_br_conv[li].astype(BF16), sig, tm=tm_mid, tn=_div_tile(d, 512, LANE))
        (a,) = _matmul(merged, w_out[li].astype(BF16), tm=tm_big, tn=_div_tile(d, 512, LANE), outs=[(d, F32)],
                       epilogue=_ep_store, name="w_out")
        xa, hx = _ln_res(xa, a, mods, ln1_g[li], ln1_b[li], n_ctx, alpha)

        w_r = _pad_half(w_router[li], LANE).astype(BF16)
        b_r = jnp.concatenate([b_router[li], jnp.full((LANE - n_exp,), NEG_BIG, F32)]).reshape(1, LANE)
        top_idx, top_w = _router(hx, w_r, b_r)
        tile_expert, n_used, row_token, pos = _routing_tables(top_idx[:, :TOP_K], n_exp, MOE_TILE)
        wgu = w_gate_up[li]
        wgu_p = jnp.concatenate([wgu[:, :, 0::2], wgu[:, :, 1::2]], axis=2).astype(BF16)
        bgu = b_gate_up[li]
        bgu_p = jnp.concatenate([bgu[:, 0::2], bgu[:, 1::2]], axis=1)
        y_sorted = _moe_ffn(hx, tile_expert, n_used, row_token, wgu_p, bgu_p, w_down[li].astype(BF16), b_down[li])
        next_mods = mods_all[li + 1] if li + 1 < depth else None
        xa, h = _moe_combine(pos, y_sorted, top_w, xa, mods, ln2_g[li], ln2_b[li], next_mods, n_ctx, alpha)

    return xa[n_ctx:].reshape(bsz, s, d)
```

```python
import functools

import jax
import jax.numpy as jnp
from jax import lax
from jax.experimental import pallas as pl
from jax.experimental.pallas import tpu as pltpu

GRID_W = 64
ROPE_BASE = 10000.0
MLA_HEADS = 8
MLA_NOPE = 128
MLA_ROPE = 64
MLA_V = 128
GQA_Q_HEADS = 16
GQA_KV_HEADS = 4
GQA_HD = 128
CONV_K = 3
N_BRANCH = 3
TOP_K = 4
SWIGLU_LIMIT = 7.0
SWIGLU_ALPHA = 1.702
NORM_EPS = 1e-6

LANE = 128
SUBLANE = 8
VMEM_LIMIT = 56 * 1024 * 1024
ROW_TILE = 256
MOE_TILE = 256
COMBINE_TILE = 64
ATTN_KEY_CHUNK = 1024
LOG2E = 1.4426950408889634
NEG_BIG = -1e30

F32 = jnp.float32
BF16 = jnp.bfloat16


def _cparams(sem, vmem=VMEM_LIMIT, **kw):
    return pltpu.CompilerParams(dimension_semantics=sem, vmem_limit_bytes=vmem, **kw)


def _div_tile(n, cap, quantum):
    best = None
    t = quantum
    while t <= min(n, cap):
        if n % t == 0:
            best = t
        t += quantum
    assert best is not None, (n, cap, quantum)
    return best


def _ada_kernel(c_ref, w_ref, b_ref, o_ref):
    c = c_ref[...]
    s = (c * jax.nn.sigmoid(c)).astype(BF16)
    o_ref[0] = jnp.dot(s, w_ref[0].astype(BF16), preferred_element_type=F32) + b_ref[0]


def _ada(cc, w_ada, b_ada):
    nl, d, n = w_ada.shape
    tn = _div_tile(n, 512, LANE)
    return pl.pallas_call(
        _ada_kernel,
        out_shape=jax.ShapeDtypeStruct((nl, SUBLANE, n), F32),
        grid=(nl, n // tn),
        in_specs=[
            pl.BlockSpec((SUBLANE, d), lambda l, j: (0, 0)),
            pl.BlockSpec((1, d, tn), lambda l, j: (l, 0, j)),
            pl.BlockSpec((1, 1, tn), lambda l, j: (l, 0, j)),
        ],
        out_specs=pl.BlockSpec((1, SUBLANE, tn), lambda l, j: (l, 0, j)),
        compiler_params=_cparams(("parallel", "parallel")),
        name="ada_mod",
    )(cc, w_ada, b_ada.reshape(nl, 1, n))


def _mm_kernel(*refs, n_rows, n_cols, epilogue):
    a_ref, b_ref = refs[0], refs[1]
    row_refs = refs[2:2 + n_rows]
    col_refs = refs[2 + n_rows:2 + n_rows + n_cols]
    out_refs = refs[2 + n_rows + n_cols:]
    acc = jnp.dot(a_ref[...], b_ref[...], preferred_element_type=F32)
    epilogue(acc, row_refs, col_refs, out_refs)


def _matmul(a, b, *, tm, tn, outs, epilogue, rows=(), cols=(), name):
    m, k = a.shape
    n = b.shape[1]
    nj = n // tn
    in_specs = [pl.BlockSpec((tm, k), lambda i, j: (i, 0)), pl.BlockSpec((k, tn), lambda i, j: (0, j))]
    for r in rows:
        in_specs.append(pl.BlockSpec((tm, r.shape[1]), lambda i, j: (i, 0)))
    for c in cols:
        in_specs.append(pl.BlockSpec((c.shape[0], tn), lambda i, j: (0, j)))
    out_shape = [jax.ShapeDtypeStruct((m, w), dt) for (w, dt) in outs]
    out_specs = [pl.BlockSpec((tm, w // nj), lambda i, j: (i, j)) for (w, dt) in outs]
    res = pl.pallas_call(
        functools.partial(_mm_kernel, n_rows=len(rows), n_cols=len(cols), epilogue=epilogue),
        out_shape=out_shape,
        grid=(m // tm, nj),
        in_specs=in_specs,
        out_specs=out_specs,
        compiler_params=_cparams(("parallel", "parallel")),
        name=name,
    )(a, b, *rows, *cols)
    return res


def _rms_rows(x, g):
    return x * lax.rsqrt(jnp.mean(x * x, axis=1, keepdims=True) + NORM_EPS) * g


def _rope_block(x, cos, sin_signed):
    return x * cos + pltpu.roll(x, LANE // 2, axis=1) * sin_signed


def _ep_store(acc, row_refs, col_refs, out_refs):
    out_refs[0][...] = acc.astype(out_refs[0].dtype)


def _ep_sigmoid(acc, row_refs, col_refs, out_refs):
    out_refs[0][...] = jax.nn.sigmoid(acc).astype(out_refs[0].dtype)


def _ep_rms(acc, row_refs, col_refs, out_refs):
    out_refs[0][...] = _rms_rows(acc, col_refs[0][...]).astype(out_refs[0].dtype)


def _ep_kvpe(acc, row_refs, col_refs, out_refs, *, kvl):
    out_refs[0][...] = _rms_rows(acc[:, :kvl], col_refs[0][:, :kvl]).astype(out_refs[0].dtype)
    out_refs[1][...] = _rope_block(acc[:, kvl:], row_refs[0][...], row_refs[1][...]).astype(out_refs[1].dtype)


def _ep_gqk(acc, row_refs, col_refs, out_refs):
    cos = row_refs[0][...]
    sin = row_refs[1][...]
    gain = col_refs[0][...]
    scale = col_refs[1][...]
    for h in range(acc.shape[1] // GQA_HD):
        sl = slice(h * GQA_HD, (h + 1) * GQA_HD)
        y = _rms_rows(acc[:, sl], gain[:, sl])
        out_refs[0][:, sl] = (_rope_block(y, cos, sin) * scale[:, sl]).astype(out_refs[0].dtype)


def _ep_qup(acc, row_refs, col_refs, out_refs, *, qscale):
    cos = row_refs[0][...]
    sin = row_refs[1][...]
    for h in range(MLA_HEADS):
        a0 = h * 2 * LANE
        out_refs[0][:, a0:a0 + LANE] = (acc[:, a0:a0 + LANE] * qscale).astype(out_refs[0].dtype)
        r = _rope_block(acc[:, a0 + LANE:a0 + 2 * LANE], cos, sin) * qscale
        out_refs[0][:, a0 + LANE:a0 + 2 * LANE] = r.astype(out_refs[0].dtype)


def _ep_kvup(acc, row_refs, col_refs, out_refs):
    kpe = row_refs[0][...]
    hw = MLA_HEADS * MLA_NOPE
    for h in range(MLA_HEADS):
        out_refs[0][:, h * 2 * LANE:h * 2 * LANE + LANE] = acc[:, h * LANE:(h + 1) * LANE].astype(out_refs[0].dtype)
        out_refs[0][:, h * 2 * LANE + LANE:(h + 1) * 2 * LANE] = kpe
    out_refs[1][...] = acc[:, hw:].astype(out_refs[1].dtype)


def _mod_index(n_ctx_tiles):
    return lambda i: (jnp.where(i < n_ctx_tiles, 1, 0), 0, 0)


def _modulate_kernel(x_ref, mod_ref, h_ref):
    m = mod_ref[0]
    h_ref[...] = (x_ref[...] * (1.0 + m[1:2, :]) + m[0:1, :]).astype(h_ref.dtype)


def _modulate(xa, mods, n_ctx):
    t, d = xa.shape
    return pl.pallas_call(
        _modulate_kernel,
        out_shape=jax.ShapeDtypeStruct((t, d), BF16),
        grid=(t // ROW_TILE,),
        in_specs=[pl.BlockSpec((ROW_TILE, d), lambda i: (i, 0)),
                  pl.BlockSpec((1, 6, d), _mod_index(n_ctx // ROW_TILE))],
        out_specs=pl.BlockSpec((ROW_TILE, d), lambda i: (i, 0)),
        compiler_params=_cparams(("parallel",)),
        name="modulate",
    )(xa, mods)


def _layernorm_rows(v, g, b):
    mu = jnp.mean(v, axis=1, keepdims=True)
    vc = v - mu
    var = jnp.mean(vc * vc, axis=1, keepdims=True)
    return vc * lax.rsqrt(var + NORM_EPS) * g + b


def _ln_res_kernel(x_ref, a_ref, mod_ref, g_ref, b_ref, y_ref, h_ref, *, alpha):
    m = mod_ref[0]
    y = _layernorm_rows(alpha * x_ref[...] + m[2:3, :] * a_ref[...], g_ref[...], b_ref[...])
    y_ref[...] = y
    h_ref[...] = y * (1.0 + m[4:5, :]) + m[3:4, :]


def _ln_res(xa, a, mods, g, b, n_ctx, alpha):
    t, d = xa.shape
    row = pl.BlockSpec((ROW_TILE, d), lambda i: (i, 0))
    vec = pl.BlockSpec((1, d), lambda i: (0, 0))
    return pl.pallas_call(
        functools.partial(_ln_res_kernel, alpha=alpha),
        out_shape=[jax.ShapeDtypeStruct((t, d), F32), jax.ShapeDtypeStruct((t, d), F32)],
        grid=(t // ROW_TILE,),
        in_specs=[row, row, pl.BlockSpec((1, 6, d), _mod_index(n_ctx // ROW_TILE)), vec, vec],
        out_specs=[row, row],
        compiler_params=_cparams(("parallel",)),
        name="ln_res",
    )(xa, a, mods, g.reshape(1, d), b.reshape(1, d))


def _conv_kernel(z_ref, zp_ref, zn_ref, w_ref, o_ref, *, cw, ctx_tiles, n_tiles):
    i = pl.program_id(0)
    starts = jnp.logical_or(i == 0, i == ctx_tiles)
    ends = jnp.logical_or(i == ctx_tiles - 1, i == n_tiles - 1)
    cb = z_ref[:, 0:cw]
    u = z_ref[:, cw:2 * cw] * z_ref[:, 2 * cw:3 * cw]
    up = zp_ref[SUBLANE - 1:SUBLANE, cw:2 * cw] * zp_ref[SUBLANE - 1:SUBLANE, 2 * cw:3 * cw]
    un = zn_ref[0:1, cw:2 * cw] * zn_ref[0:1, 2 * cw:3 * cw]
    up = jnp.where(starts, 0.0, up)
    un = jnp.where(ends, 0.0, un)
    rows = u.shape[0]
    rid = lax.broadcasted_iota(jnp.int32, u.shape, 0)
    u_prev = jnp.where(rid == 0, up, pltpu.roll(u, 1, axis=0))
    u_next = jnp.where(rid == rows - 1, un, pltpu.roll(u, rows - 1, axis=0))
    y = w_ref[0:1, :] * u_prev + w_ref[1:2, :] * u + w_ref[2:3, :] * u_next
    o_ref[...] = (cb * y).astype(o_ref.dtype)


def _short_conv(zc, conv_w, n_ctx):
    t, w3 = zc.shape
    cw = w3 // 3
    n_tiles = t // ROW_TILE
    per = ROW_TILE // SUBLANE
    last8 = t // SUBLANE - 1
    return pl.pallas_call(
        functools.partial(_conv_kernel, cw=cw, ctx_tiles=n_ctx // ROW_TILE, n_tiles=n_tiles),
        out_shape=jax.ShapeDtypeStruct((t, cw), BF16),
        grid=(n_tiles,),
        in_specs=[
            pl.BlockSpec((ROW_TILE, w3), lambda i: (i, 0)),
            pl.BlockSpec((SUBLANE, w3), lambda i: (jnp.maximum(i * per - 1, 0), 0)),
            pl.BlockSpec((SUBLANE, w3), lambda i: (jnp.minimum((i + 1) * per, last8), 0)),
            pl.BlockSpec((CONV_K, cw), lambda i: (0, 0)),
        ],
        out_specs=pl.BlockSpec((ROW_TILE, cw), lambda i: (i, 0)),
        compiler_params=_cparams(("parallel",)),
        name="short_conv",
    )(zc, zc, zc, conv_w)


def _router_kernel(h_ref, w_ref, b_ref, idx_ref, wt_ref):
    logits = jnp.dot(h_ref[...].astype(BF16), w_ref[...], preferred_element_type=F32) + b_ref[...]
    lane = lax.broadcasted_iota(jnp.int32, logits.shape, 1)
    vals, idxs = [], []
    cur = logits
    for _ in range(TOP_K):
        mx = jnp.max(cur, axis=1, keepdims=True)
        ix = jnp.min(jnp.where(cur == mx, lane, LANE), axis=1, keepdims=True)
        vals.append(mx)
        idxs.append(ix)
        cur = jnp.where(lane == ix, -jnp.inf, cur)
    es = [jnp.exp(v - vals[0]) for v in vals]
    tot = es[0]
    for e in es[1:]:
        tot = tot + e
    idx_out = jnp.zeros(logits.shape, jnp.int32)
    wt_out = jnp.zeros(logits.shape, F32)
    for k in range(TOP_K):
        idx_out = jnp.where(lane == k, idxs[k], idx_out)
        wt_out = jnp.where(lane == k, es[k] / tot, wt_out)
    idx_ref[...] = idx_out
    wt_ref[...] = wt_out


def _router(hx, w_router_p, b_router_p):
    t, d = hx.shape
    out = pl.BlockSpec((ROW_TILE, LANE), lambda i: (i, 0))
    return pl.pallas_call(
        _router_kernel,
        out_shape=[jax.ShapeDtypeStruct((t, LANE), jnp.int32), jax.ShapeDtypeStruct((t, LANE), F32)],
        grid=(t // ROW_TILE,),
        in_specs=[pl.BlockSpec((ROW_TILE, d), lambda i: (i, 0)),
                  pl.BlockSpec((d, LANE), lambda i: (0, 0)),
                  pl.BlockSpec((1, LANE), lambda i: (0, 0))],
        out_specs=[out, out],
        compiler_params=_cparams(("parallel",)),
        name="router",
    )(hx, w_router_p, b_router_p)


def _attn_kernel(q_ref, k_ref, vt_ref, o_ref, s_a, s_b, m_scr, l_scr, acc_scr, *, groups, dk, dv, tk, n_ctx,
                 n_lat_chunks, ctx_tiles):
    i = pl.program_id(1)
    tq = q_ref.shape[0]
    if groups > 1:
        q = jnp.concatenate([q_ref[:, g * dk:(g + 1) * dk] for g in range(groups)], axis=0)
    else:
        q = q_ref[...]
    nt_dims = (((1,), (1,)), ((), ()))

    def scores(off, size):
        return lax.dot_general(k_ref[pl.ds(off, size), :], q, nt_dims, preferred_element_type=F32)

    def update(s, off, m, l):
        m_new = jnp.maximum(m, jnp.max(s, axis=0, keepdims=True))
        p = jnp.exp2(s - m_new)
        alpha = jnp.exp2(m - m_new)
        l_new = alpha * l + jnp.sum(p, axis=0, keepdims=True)
        pv = jnp.dot(vt_ref[:, pl.ds(off, tk)], p.astype(BF16), preferred_element_type=F32)
        acc_scr[...] = alpha * acc_scr[...] + pv
        return m_new, l_new

    s0 = scores(0, n_ctx)
    m0 = jnp.max(s0, axis=0, keepdims=True)
    p0 = jnp.exp2(s0 - m0)
    m_scr[...] = m0
    l_scr[...] = jnp.sum(p0, axis=0, keepdims=True)
    acc_scr[...] = jnp.dot(vt_ref[:, 0:n_ctx], p0.astype(BF16), preferred_element_type=F32)

    @pl.when(i >= ctx_tiles)
    def _():
        s_a[...] = scores(n_ctx, tk)

        def pair(base, m, l, prefetch):
            s_b[...] = scores(base + tk, tk)
            m, l = update(s_a[...], base, m, l)
            if prefetch:
                s_a[...] = scores(base + 2 * tk, tk)
            return update(s_b[...], base + tk, m, l)

        def body(c, carry):
            base = pl.multiple_of(n_ctx + c * (2 * tk), LANE)
            return pair(base, carry[0], carry[1], True)

        m, l = lax.fori_loop(0, n_lat_chunks // 2 - 1, body, (m_scr[...], l_scr[...]))
        m, l = pair(n_ctx + (n_lat_chunks - 2) * tk, m, l, False)
        m_scr[...] = m
        l_scr[...] = l

    o_t = acc_scr[...] / l_scr[...]
    for g in range(groups):
        o_ref[:, g * dv:(g + 1) * dv] = o_t[:, g * tq:(g + 1) * tq].T.astype(o_ref.dtype)


def _attention(q, k, vt, *, n_kv_heads, groups, dk, dv, tq, tk, n_ctx, k_col0=0):
    t = q.shape[0]
    n_lat = t - n_ctx
    rows = groups * tq
    n_lat_chunks = n_lat // tk
    assert n_lat_chunks % 2 == 0 and n_lat_chunks >= 2
    return pl.pallas_call(
        functools.partial(_attn_kernel, groups=groups, dk=dk, dv=dv, tk=tk, n_ctx=n_ctx,
                          n_lat_chunks=n_lat_chunks, ctx_tiles=n_ctx // tq),
        out_shape=jax.ShapeDtypeStruct((t, n_kv_heads * groups * dv), BF16),
        grid=(n_kv_heads, t // tq),
        in_specs=[
            pl.BlockSpec((tq, groups * dk), lambda h, i: (i, h)),
            pl.BlockSpec((t, dk), lambda h, i: (0, k_col0 + h)),
            pl.BlockSpec((dv, t), lambda h, i: (h, 0)),
        ],
        out_specs=pl.BlockSpec((tq, groups * dv), lambda h, i: (i, h)),
        scratch_shapes=[pltpu.VMEM((tk, rows), F32), pltpu.VMEM((tk, rows), F32), pltpu.VMEM((1, rows), F32),
                        pltpu.VMEM((1, rows), F32), pltpu.VMEM((dv, rows), F32)],
        compiler_params=_cparams(("parallel", "parallel")),
        name="prefix_attention",
    )(q, k, vt)


def _branch_kernel(m_ref, g_ref, c_ref, wm_ref, wg_ref, wc_ref, s0_ref, s1_ref, s2_ref, o_ref):
    acc = s0_ref[...] * jnp.dot(m_ref[...], wm_ref[...], preferred_element_type=F32)
    acc = acc + s1_ref[...] * jnp.dot(g_ref[...], wg_ref[...], preferred_element_type=F32)
    acc = acc + s2_ref[...] * jnp.dot(c_ref[...], wc_ref[...], preferred_element_type=F32)
    o_ref[...] = acc.astype(o_ref.dtype)


def _branch_merge(mla, gqa, conv, wm, wg, wc, sig, *, tm, tn):
    t = mla.shape[0]
    d = wm.shape[1]
    nj = d // tn

    def rowspec(a):
        return pl.BlockSpec((tm, a.shape[1]), lambda i, j: (i, 0))

    def wspec(w):
        return pl.BlockSpec((w.shape[0], tn), lambda i, j: (0, j))

    def sigspec(b):
        return pl.BlockSpec((tm, tn), lambda i, j: (i, j + b * nj))

    return pl.pallas_call(
        _branch_kernel,
        out_shape=jax.ShapeDtypeStruct((t, d), BF16),
        grid=(t // tm, nj),
        in_specs=[rowspec(mla), rowspec(gqa), rowspec(conv), wspec(wm), wspec(wg), wspec(wc),
                  sigspec(0), sigspec(1), sigspec(2)],
        out_specs=pl.BlockSpec((tm, tn), lambda i, j: (i, j)),
        compiler_params=_cparams(("parallel", "parallel")),
        name="branch_merge",
    )(mla, gqa, conv, wm, wg, wc, sig, sig, sig)


def _moe_ffn_kernel(te_ref, nt_ref, rt_ref, h_hbm, wgu_ref, bgu_ref, wd_ref, bd_ref, y_ref, buf, sem, *, tm, de):
    t = pl.program_id(0)
    n_used = nt_ref[0]

    def row_copy(tile, slot, r):
        tok = rt_ref[tile * tm + r]
        return pltpu.make_async_copy(h_hbm.at[pl.ds(tok, 1), :], buf.at[slot, pl.ds(r, 1), :], sem.at[slot])

    def start_gather(tile, slot):
        def body(r, carry):
            row_copy(tile, slot, r).start()
            return carry
        lax.fori_loop(0, tm, body, 0)

    def wait_gather(tile, slot):
        def body(r, carry):
            row_copy(tile, slot, r).wait()
            return carry
        lax.fori_loop(0, tm, body, 0)

    @pl.when(t == 0)
    def _():
        start_gather(0, 0)

    @pl.when(t + 1 < n_used)
    def _():
        start_gather(t + 1, (t + 1) % 2)

    @pl.when(t < n_used)
    def _():
        slot = t % 2
        wait_gather(t, slot)
        xs = buf[slot].astype(BF16)
        gu = jnp.dot(xs, wgu_ref[0], preferred_element_type=F32) + bgu_ref[0]
        gate = jnp.minimum(gu[:, :de], SWIGLU_LIMIT)
        up = jnp.clip(gu[:, de:], -SWIGLU_LIMIT, SWIGLU_LIMIT)
        act = (up + 1.0) * (gate * jax.nn.sigmoid(SWIGLU_ALPHA * gate))
        y_ref[...] = jnp.dot(act.astype(BF16), wd_ref[0], preferred_element_type=F32) + bd_ref[0]

    @pl.when(t >= n_used)
    def _():
        y_ref[...] = jnp.zeros_like(y_ref)


def _moe_ffn(hx, tile_expert, n_used, row_token, wgu, bgu, wd, bd):
    t, d = hx.shape
    n_exp, _, de2 = wgu.shape
    de = de2 // 2
    n_tiles = tile_expert.shape[0]
    tm = MOE_TILE
    grid_spec = pltpu.PrefetchScalarGridSpec(
        num_scalar_prefetch=3,
        grid=(n_tiles,),
        in_specs=[
            pl.BlockSpec(memory_space=pl.ANY),
            pl.BlockSpec((1, d, de2), lambda i, te, nt, rt: (te[i], 0, 0)),
            pl.BlockSpec((1, 1, de2), lambda i, te, nt, rt: (te[i], 0, 0)),
            pl.BlockSpec((1, de, d), lambda i, te, nt, rt: (te[i], 0, 0)),
            pl.BlockSpec((1, 1, d), lambda i, te, nt, rt: (te[i], 0, 0)),
        ],
        out_specs=pl.BlockSpec((tm, d), lambda i, te, nt, rt: (i, 0)),
        scratch_shapes=[pltpu.VMEM((2, tm, d), F32), pltpu.SemaphoreType.DMA((2,))],
    )
    return pl.pallas_call(
        functools.partial(_moe_ffn_kernel, tm=tm, de=de),
        out_shape=jax.ShapeDtypeStruct((n_tiles * tm, d), F32),
        grid_spec=grid_spec,
        compiler_params=_cparams(("arbitrary",)),
        name="moe_ffn",
    )(tile_expert, n_used, row_token, hx, wgu, bgu.reshape(n_exp, 1, de2), wd, bd.reshape(n_exp, 1, d))


def _combine_kernel(pos_ref, y_hbm, wt_ref, x_ref, mod_ref, g_ref, b_ref, *rest, tt, alpha, with_next):
    if with_next:
        nmod_ref, o_ref, h_ref, buf, sem = rest
    else:
        o_ref, buf, sem = rest
    i = pl.program_id(0)
    n = pl.num_programs(0)

    def row_copy(tile, slot, r, k):
        src = pos_ref[(tile * tt + r) * TOP_K + k]
        return pltpu.make_async_copy(y_hbm.at[pl.ds(src, 1), :], buf.at[slot, k, pl.ds(r, 1), :], sem.at[slot])

    def start_gather(tile, slot):
        def body(r, carry):
            for k in range(TOP_K):
                row_copy(tile, slot, r, k).start()
            return carry
        lax.fori_loop(0, tt, body, 0)

    def wait_gather(tile, slot):
        def body(r, carry):
            for k in range(TOP_K):
                row_copy(tile, slot, r, k).wait()
            return carry
        lax.fori_loop(0, tt, body, 0)

    @pl.when(i == 0)
    def _():
        start_gather(0, 0)

    @pl.when(i + 1 < n)
    def _():
        start_gather(i + 1, (i + 1) % 2)

    slot = i % 2
    wait_gather(i, slot)
    wt = wt_ref[...]
    f = wt[:, 0:1] * buf[slot, 0]
    for k in range(1, TOP_K):
        f = f + wt[:, k:k + 1] * buf[slot, k]
    m = mod_ref[0]
    y = _layernorm_rows(alpha * x_ref[...] + m[5:6, :] * f, g_ref[...], b_ref[...])
    o_ref[...] = y
    if with_next:
        nm = nmod_ref[0]
        h_ref[...] = (y * (1.0 + nm[1:2, :]) + nm[0:1, :]).astype(h_ref.dtype)


def _moe_combine(pos, y_sorted, wts, xa, mods, g, b, next_mods, n_ctx, alpha):
    t, d = xa.shape
    tt = COMBINE_TILE
    with_next = next_mods is not None
    ctx_tiles = n_ctx // tt
    modspec = pl.BlockSpec((1, 6, d), lambda i, pos: (jnp.where(i < ctx_tiles, 1, 0), 0, 0))
    row = pl.BlockSpec((tt, d), lambda i, pos: (i, 0))
    vec = pl.BlockSpec((1, d), lambda i, pos: (0, 0))
    in_specs = [pl.BlockSpec(memory_space=pl.ANY), pl.BlockSpec((tt, LANE), lambda i, pos: (i, 0)), row, modspec,
                vec, vec]
    args = [pos, y_sorted, wts, xa, mods, g.reshape(1, d), b.reshape(1, d)]
    out_shape = [jax.ShapeDtypeStruct((t, d), F32)]
    out_specs = [row]
    if with_next:
        in_specs.append(modspec)
        args.append(next_mods)
        out_shape.append(jax.ShapeDtypeStruct((t, d), BF16))
        out_specs.append(row)
    grid_spec = pltpu.PrefetchScalarGridSpec(
        num_scalar_prefetch=1,
        grid=(t // tt,),
        in_specs=in_specs,
        out_specs=out_specs,
        scratch_shapes=[pltpu.VMEM((2, TOP_K, tt, d), F32), pltpu.SemaphoreType.DMA((2,))],
    )
    res = pl.pallas_call(
        functools.partial(_combine_kernel, tt=tt, alpha=alpha, with_next=with_next),
        out_shape=out_shape,
        grid_spec=grid_spec,
        compiler_params=_cparams(("arbitrary",)),
        name="moe_combine_ln",
    )(*args)
    return res if with_next else (res[0], None)


def _permcast_kernel(w_ref, p_ref, o_ref):
    o_ref[0] = jnp.dot(w_ref[0].astype(BF16), p_ref[...], preferred_element_type=F32).astype(o_ref.dtype)


def _deinterleave_cols(w):
    e, k, n = w.shape
    src_col = jnp.concatenate([jnp.arange(0, n, 2), jnp.arange(1, n, 2)])
    perm = (jnp.arange(n)[:, None] == src_col[None, :]).astype(BF16)
    tk = _div_tile(k, 1024, 2 * SUBLANE)
    return pl.pallas_call(
        _permcast_kernel,
        out_shape=jax.ShapeDtypeStruct((e, k, n), BF16),
        grid=(e, k // tk),
        in_specs=[pl.BlockSpec((1, tk, n), lambda ei, i: (ei, i, 0)), pl.BlockSpec((n, n), lambda ei, i: (0, 0))],
        out_specs=pl.BlockSpec((1, tk, n), lambda ei, i: (ei, i, 0)),
        compiler_params=_cparams(("parallel", "parallel")),
        name="deinterleave_cols",
    )(w, perm)


def _routing_tables(top_idx, n_exp, tm):
    t = top_idx.shape[0]
    n_pairs = t * TOP_K
    n_tiles = -(-(n_pairs + n_exp * (tm - 1)) // tm)
    flat_e = top_idx.reshape(-1)
    order = jnp.argsort(flat_e, stable=True).astype(jnp.int32)
    e_sorted = flat_e[order]
    sizes = jnp.bincount(flat_e, length=n_exp).astype(jnp.int32)
    padded = ((sizes + tm - 1) // tm) * tm
    pend = jnp.cumsum(padded)
    pstart = pend - padded
    gstart = jnp.cumsum(sizes) - sizes
    dest = pstart[e_sorted] + (jnp.arange(n_pairs, dtype=jnp.int32) - gstart[e_sorted])
    row_token = jnp.zeros((n_tiles * tm,), jnp.int32).at[dest].set(order // TOP_K)
    pos = jnp.zeros((n_pairs,), jnp.int32).at[order].set(dest)
    tile_start = jnp.arange(n_tiles, dtype=jnp.int32) * tm
    tile_expert = jnp.minimum(jnp.searchsorted(pend, tile_start, side="right"), n_exp - 1).astype(jnp.int32)
    n_used = (pend[-1] // tm).astype(jnp.int32).reshape(1)
    return tile_expert, n_used, row_token, pos


def _rope_tables(n_ctx, n_lat, rot_dim):
    rows = n_lat // GRID_W
    row = jnp.repeat(jnp.arange(rows, dtype=F32), GRID_W)
    col = jnp.tile(jnp.arange(GRID_W, dtype=F32), rows)
    nf = rot_dim // 4
    inv = ROPE_BASE ** (-jnp.arange(nf, dtype=F32) / nf)
    ang = jnp.concatenate([row[:, None] * inv, col[:, None] * inv], axis=-1)
    ang = jnp.concatenate([jnp.zeros((n_ctx, rot_dim // 2), F32), ang], axis=0)
    return jnp.cos(ang), jnp.sin(ang)


def _pad_half(a, width):
    return jnp.pad(a, ((0, 0), (0, width - a.shape[1])))


def kernel(x, c, ctx, c_ctx, w_ada, b_ada, w_in, g_q_lat, g_kv_lat, w_q_up, w_kv_up, g_q_head, g_k_head, conv_w,
           w_br_mla, w_br_gqa, w_br_conv, w_out, ln1_g, ln1_b, w_router, b_router, w_gate_up, b_gate_up, w_down,
           b_down, ln2_g, ln2_b):
    bsz, s, d = x.shape
    assert bsz == 1, "single-sample prefill"
    n_ctx = ctx.shape[1]
    t = n_ctx + s
    depth = w_ada.shape[0]
    ql = g_q_lat.shape[1]
    kvl = g_kv_lat.shape[1]
    cw = conv_w.shape[2]
    n_exp = w_router.shape[2]
    de = w_down.shape[2]
    alpha = (2 * depth) ** 0.25
    assert n_ctx % ROW_TILE == 0 and s % ROW_TILE == 0 and n_exp <= LANE
    half = MLA_ROPE // 2

    cos_a, sin_a = _rope_tables(n_ctx, s, MLA_ROPE)
    cos_a_p = jnp.concatenate([_pad_half(cos_a, LANE // 2), _pad_half(cos_a, LANE // 2)], axis=1)
    sin_a_p = jnp.concatenate([_pad_half(-sin_a, LANE // 2), _pad_half(sin_a, LANE // 2)], axis=1)
    cos_b, sin_b = _rope_tables(n_ctx, s, GQA_HD)
    cos_b_p = jnp.concatenate([cos_b, cos_b], axis=1)
    sin_b_p = jnp.concatenate([-sin_b, sin_b], axis=1)

    cc = jnp.zeros((SUBLANE, d), F32).at[0].set(c[0]).at[1].set(c_ctx)
    mods_all = _ada(cc, w_ada, b_ada).reshape(depth, SUBLANE, 6, d)[:, :2]

    xa = jnp.concatenate([ctx[0], x[0]], axis=0)
    h = _modulate(xa, mods_all[0], n_ctx)

    tm_big = _div_tile(t, 1280, ROW_TILE)
    tm_mid = _div_tile(t, 640, SUBLANE * 16)
    q_scale_mla = (MLA_NOPE + MLA_ROPE) ** -0.5 * LOG2E
    q_scale_gqa = GQA_HD ** -0.5 * LOG2E

    for li in range(depth):
        mods = mods_all[li]
        wi = w_in[li]
        o = 0
        w_qlat = wi[:, o:o + ql].astype(BF16); o += ql
        w_kvlat = wi[:, o:o + kvl]; o += kvl
        w_kpe = wi[:, o:o + MLA_ROPE]; o += MLA_ROPE
        w_kpe_p = jnp.concatenate([_pad_half(w_kpe[:, :half], LANE // 2), _pad_half(w_kpe[:, half:], LANE // 2)], 1)
        w_kvpe = jnp.concatenate([w_kvlat, w_kpe_p], axis=1).astype(BF16)
        n_gqk = (GQA_Q_HEADS + GQA_KV_HEADS) * GQA_HD
        w_gqk = wi[:, o:o + n_gqk].astype(BF16); o += n_gqk
        n_gv = GQA_KV_HEADS * GQA_HD
        w_gv = wi[:, o:o + n_gv].astype(BF16); o += n_gv
        w_cv = wi[:, o:o + 3 * cw].astype(BF16); o += 3 * cw
        w_gt = wi[:, o:o + N_BRANCH * d].astype(BF16); o += N_BRANCH * d
        assert o == wi.shape[1]

        wq = w_q_up[li].reshape(ql, MLA_HEADS, MLA_NOPE + MLA_ROPE)
        zq = jnp.zeros((ql, MLA_HEADS, LANE // 2 - half), wq.dtype)
        wq_p = jnp.concatenate([wq[:, :, :MLA_NOPE], wq[:, :, MLA_NOPE:MLA_NOPE + half], zq,
                                wq[:, :, MLA_NOPE + half:], zq], axis=2).reshape(ql, MLA_HEADS * 2 * LANE)
        wq_p = wq_p.astype(BF16)
        wkv = w_kv_up[li].reshape(kvl, MLA_HEADS, MLA_NOPE + MLA_V)
        wkv_p = jnp.concatenate([wkv[:, :, :MLA_NOPE].reshape(kvl, -1), wkv[:, :, MLA_NOPE:].reshape(kvl, -1)],
                                axis=1).astype(BF16)

        gain_qk = jnp.concatenate([jnp.tile(g_q_head[li], GQA_Q_HEADS), jnp.tile(g_k_head[li], GQA_KV_HEADS)])
        scale_qk = jnp.concatenate([jnp.full((GQA_Q_HEADS * GQA_HD,), q_scale_gqa, F32),
                                    jnp.ones((GQA_KV_HEADS * GQA_HD,), F32)])
        g_kvpe = jnp.concatenate([g_kv_lat[li], jnp.ones((LANE,), F32)]).reshape(1, kvl + LANE)

        (qn,) = _matmul(h, w_qlat, tm=tm_mid, tn=ql, outs=[(ql, BF16)], epilogue=_ep_rms,
                        cols=[g_q_lat[li].reshape(1, ql)], name="in_qlat")
        kvn, kpe = _matmul(h, w_kvpe, tm=tm_mid, tn=kvl + LANE, outs=[(kvl, BF16), (LANE, BF16)],
                           epilogue=functools.partial(_ep_kvpe, kvl=kvl), rows=[cos_a_p, sin_a_p], cols=[g_kvpe],
                           name="in_kvpe")
        (gqk,) = _matmul(h, w_gqk, tm=tm_big, tn=4 * GQA_HD, outs=[(n_gqk, BF16)], epilogue=_ep_gqk,
                         rows=[cos_b_p, sin_b_p], cols=[gain_qk.reshape(1, -1), scale_qk.reshape(1, -1)],
                         name="in_gqk")
        (gv,) = _matmul(h, w_gv, tm=tm_big, tn=n_gv, outs=[(n_gv, BF16)], epilogue=_ep_store, name="in_gv")
        (zc,) = _matmul(h, w_cv, tm=tm_big, tn=_div_tile(3 * cw, 512, LANE), outs=[(3 * cw, F32)],
                        epilogue=_ep_store, name="in_conv")
        (sig,) = _matmul(h, w_gt, tm=tm_big, tn=_div_tile(N_BRANCH * d, 512, LANE), outs=[(N_BRANCH * d, F32)],
                         epilogue=_ep_sigmoid, name="in_gates")

        (q_mla,) = _matmul(qn, wq_p, tm=tm_big, tn=MLA_HEADS * 2 * LANE, outs=[(MLA_HEADS * 2 * LANE, BF16)],
                           epilogue=functools.partial(_ep_qup, qscale=q_scale_mla), rows=[cos_a_p, sin_a_p],
                           name="mla_q_up")
        k_mla, v_mla = _matmul(kvn, wkv_p, tm=tm_big, tn=MLA_HEADS * 2 * LANE,
                               outs=[(MLA_HEADS * 2 * LANE, BF16), (MLA_HEADS * MLA_V, BF16)], epilogue=_ep_kvup,
                               rows=[kpe], name="mla_kv_up")
        tk = _div_tile(s // 2, ATTN_KEY_CHUNK, ROW_TILE)
        mla = _attention(q_mla, k_mla, v_mla.T, n_kv_heads=MLA_HEADS, groups=1, dk=2 * LANE, dv=MLA_V, tq=ROW_TILE,
                         tk=tk, n_ctx=n_ctx)

        gqa = _attention(gqk, gqk, gv.T, n_kv_heads=GQA_KV_HEADS, groups=GQA_Q_HEADS // GQA_KV_HEADS, dk=GQA_HD,
                         dv=GQA_HD, tq=ROW_TILE // 2, tk=tk, n_ctx=n_ctx, k_col0=GQA_Q_HEADS)

        conv = _short_conv(zc, conv_w[li], n_ctx)

        merged = _branch_merge(mla, gqa, conv, w_br_mla[li].astype(BF16), w_br_gqa[li].astype(BF16),
                               w_br_conv[li].astype(BF16), sig, tm=tm_mid, tn=_div_tile(d, 512, LANE))
        (a,) = _matmul(merged, w_out[li].astype(BF16), tm=tm_big, tn=_div_tile(d, 512, LANE), outs=[(d, F32)],
                       epilogue=_ep_store, name="w_out")
        xa, hx = _ln_res(xa, a, mods, ln1_g[li], ln1_b[li], n_ctx, alpha)

        w_r = _pad_half(w_router[li], LANE).astype(BF16)
        b_r = jnp.concatenate([b_router[li], jnp.full((LANE - n_exp,), NEG_BIG, F32)]).reshape(1, LANE)
        top_idx, top_w = _router(hx, w_r, b_r)
        tile_expert, n_used, row_token, pos = _routing_tables(top_idx[:, :TOP_K], n_exp, MOE_TILE)
        wgu_p = _deinterleave_cols(w_gate_up[li])
        bgu = b_gate_up[li]
        bgu_p = jnp.concatenate([bgu[:, 0::2], bgu[:, 1::2]], axis=1)
        y_sorted = _moe_ffn(hx, tile_expert, n_used, row_token, wgu_p, bgu_p, w_down[li].astype(BF16), b_down[li])
        next_mods = mods_all[li + 1] if li + 1 < depth else None
        xa, h = _moe_combine(pos, y_sorted, top_w, xa, mods, ln2_g[li], ln2_b[li], next_mods, n_ctx, alpha)

    return xa[n_ctx:].reshape(bsz, s, d)
```

```python
import functools

import jax
import jax.numpy as jnp
from jax import lax
from jax.experimental import pallas as pl
from jax.experimental.pallas import tpu as pltpu

GRID_W = 64
ROPE_BASE = 10000.0
MLA_HEADS = 8
MLA_NOPE = 128
MLA_ROPE = 64
MLA_V = 128
GQA_Q_HEADS = 16
GQA_KV_HEADS = 4
GQA_HD = 128
CONV_K = 3
N_BRANCH = 3
TOP_K = 4
SWIGLU_LIMIT = 7.0
SWIGLU_ALPHA = 1.702
NORM_EPS = 1e-6

LANE = 128
SUBLANE = 8
VMEM_LIMIT = 56 * 1024 * 1024
ROW_TILE = 256
MOE_TILE = 256
COMBINE_TILE = 64
MLA_KEY_CHUNK = 2048
GQA_KEY_CHUNK = 2048
LOG2E = 1.4426950408889634
NEG_BIG = -1e30

F32 = jnp.float32
BF16 = jnp.bfloat16


def _cparams(sem, vmem=VMEM_LIMIT, **kw):
    return pltpu.CompilerParams(dimension_semantics=sem, vmem_limit_bytes=vmem, **kw)


def _div_tile(n, cap, quantum):
    best = None
    t = quantum
    while t <= min(n, cap):
        if n % t == 0:
            best = t
        t += quantum
    assert best is not None, (n, cap, quantum)
    return best


def _ada_kernel(c_ref, w_ref, b_ref, o_ref):
    c = c_ref[...]
    s = (c * jax.nn.sigmoid(c)).astype(BF16)
    o_ref[0] = jnp.dot(s, w_ref[0].astype(BF16), preferred_element_type=F32) + b_ref[0]


def _ada(cc, w_ada, b_ada):
    nl, d, n = w_ada.shape
    tn = _div_tile(n, 512, LANE)
    return pl.pallas_call(
        _ada_kernel,
        out_shape=jax.ShapeDtypeStruct((nl, SUBLANE, n), F32),
        grid=(nl, n // tn),
        in_specs=[
            pl.BlockSpec((SUBLANE, d), lambda l, j: (0, 0)),
            pl.BlockSpec((1, d, tn), lambda l, j: (l, 0, j)),
            pl.BlockSpec((1, 1, tn), lambda l, j: (l, 0, j)),
        ],
        out_specs=pl.BlockSpec((1, SUBLANE, tn), lambda l, j: (l, 0, j)),
        compiler_params=_cparams(("parallel", "parallel")),
        name="ada_mod",
    )(cc, w_ada, b_ada.reshape(nl, 1, n))


def _mm_kernel(*refs, n_rows, n_cols, epilogue):
    a_ref, b_ref = refs[0], refs[1]
    row_refs = refs[2:2 + n_rows]
    col_refs = refs[2 + n_rows:2 + n_rows + n_cols]
    out_refs = refs[2 + n_rows + n_cols:]
    acc = jnp.dot(a_ref[...], b_ref[...], preferred_element_type=F32)
    epilogue(acc, row_refs, col_refs, out_refs)


def _matmul(a, b, *, tm, tn, outs, epilogue, rows=(), cols=(), name):
    m, k = a.shape
    n = b.shape[1]
    nj = n // tn
    in_specs = [pl.BlockSpec((tm, k), lambda i, j: (i, 0)), pl.BlockSpec((k, tn), lambda i, j: (0, j))]
    for r in rows:
        in_specs.append(pl.BlockSpec((tm, r.shape[1]), lambda i, j: (i, 0)))
    for c in cols:
        in_specs.append(pl.BlockSpec((c.shape[0], tn), lambda i, j: (0, j)))
    out_shape = [jax.ShapeDtypeStruct((m, w), dt) for (w, dt) in outs]
    out_specs = [pl.BlockSpec((tm, w // nj), lambda i, j: (i, j)) for (w, dt) in outs]
    res = pl.pallas_call(
        functools.partial(_mm_kernel, n_rows=len(rows), n_cols=len(cols), epilogue=epilogue),
        out_shape=out_shape,
        grid=(m // tm, nj),
        in_specs=in_specs,
        out_specs=out_specs,
        compiler_params=_cparams(("parallel", "parallel")),
        name=name,
    )(a, b, *rows, *cols)
    return res


def _rms_rows(x, g):
    return x * lax.rsqrt(jnp.mean(x * x, axis=1, keepdims=True) + NORM_EPS) * g


def _rope_block(x, cos, sin_signed):
    return x * cos + pltpu.roll(x, LANE // 2, axis=1) * sin_signed


def _ep_store(acc, row_refs, col_refs, out_refs):
    out_refs[0][...] = acc.astype(out_refs[0].dtype)


def _ep_sigmoid(acc, row_refs, col_refs, out_refs):
    out_refs[0][...] = jax.nn.sigmoid(acc).astype(out_refs[0].dtype)


def _ep_rms(acc, row_refs, col_refs, out_refs):
    out_refs[0][...] = _rms_rows(acc, col_refs[0][...]).astype(out_refs[0].dtype)


def _ep_kvpe(acc, row_refs, col_refs, out_refs, *, kvl):
    out_refs[0][...] = _rms_rows(acc[:, :kvl], col_refs[0][:, :kvl]).astype(out_refs[0].dtype)
    out_refs[1][...] = _rope_block(acc[:, kvl:], row_refs[0][...], row_refs[1][...]).astype(out_refs[1].dtype)


def _ep_gqk(acc, row_refs, col_refs, out_refs):
    cos = row_refs[0][...]
    sin = row_refs[1][...]
    gain = col_refs[0][...]
    scale = col_refs[1][...]
    for h in range(acc.shape[1] // GQA_HD):
        sl = slice(h * GQA_HD, (h + 1) * GQA_HD)
        y = _rms_rows(acc[:, sl], gain[:, sl])
        out_refs[0][:, sl] = (_rope_block(y, cos, sin) * scale[:, sl]).astype(out_refs[0].dtype)


def _ep_qup(acc, row_refs, col_refs, out_refs, *, qscale):
    cos = row_refs[0][...]
    sin = row_refs[1][...]
    for h in range(MLA_HEADS):
        a0 = h * 2 * LANE
        out_refs[0][:, a0:a0 + LANE] = (acc[:, a0:a0 + LANE] * qscale).astype(out_refs[0].dtype)
        r = _rope_block(acc[:, a0 + LANE:a0 + 2 * LANE], cos, sin) * qscale
        out_refs[0][:, a0 + LANE:a0 + 2 * LANE] = r.astype(out_refs[0].dtype)


def _ep_kvup(acc, row_refs, col_refs, out_refs):
    kpe = row_refs[0][...]
    hw = MLA_HEADS * MLA_NOPE
    for h in range(MLA_HEADS):
        out_refs[0][:, h * 2 * LANE:h * 2 * LANE + LANE] = acc[:, h * LANE:(h + 1) * LANE].astype(out_refs[0].dtype)
        out_refs[0][:, h * 2 * LANE + LANE:(h + 1) * 2 * LANE] = kpe
    out_refs[1][...] = acc[:, hw:].astype(out_refs[1].dtype)


def _mod_index(n_ctx_tiles):
    return lambda i: (jnp.where(i < n_ctx_tiles, 1, 0), 0, 0)


def _modulate_kernel(x_ref, mod_ref, h_ref):
    m = mod_ref[0]
    h_ref[...] = (x_ref[...] * (1.0 + m[1:2, :]) + m[0:1, :]).astype(h_ref.dtype)


def _modulate(xa, mods, n_ctx):
    t, d = xa.shape
    return pl.pallas_call(
        _modulate_kernel,
        out_shape=jax.ShapeDtypeStruct((t, d), BF16),
        grid=(t // ROW_TILE,),
        in_specs=[pl.BlockSpec((ROW_TILE, d), lambda i: (i, 0)),
                  pl.BlockSpec((1, 6, d), _mod_index(n_ctx // ROW_TILE))],
        out_specs=pl.BlockSpec((ROW_TILE, d), lambda i: (i, 0)),
        compiler_params=_cparams(("parallel",)),
        name="modulate",
    )(xa, mods)


def _layernorm_rows(v, g, b):
    mu = jnp.mean(v, axis=1, keepdims=True)
    vc = v - mu
    var = jnp.mean(vc * vc, axis=1, keepdims=True)
    return vc * lax.rsqrt(var + NORM_EPS) * g + b


def _ln_res_kernel(x_ref, a_ref, mod_ref, g_ref, b_ref, y_ref, h_ref, *, alpha):
    m = mod_ref[0]
    y = _layernorm_rows(alpha * x_ref[...] + m[2:3, :] * a_ref[...], g_ref[...], b_ref[...])
    y_ref[...] = y
    h_ref[...] = y * (1.0 + m[4:5, :]) + m[3:4, :]


def _ln_res(xa, a, mods, g, b, n_ctx, alpha):
    t, d = xa.shape
    row = pl.BlockSpec((ROW_TILE, d), lambda i: (i, 0))
    vec = pl.BlockSpec((1, d), lambda i: (0, 0))
    return pl.pallas_call(
        functools.partial(_ln_res_kernel, alpha=alpha),
        out_shape=[jax.ShapeDtypeStruct((t, d), F32), jax.ShapeDtypeStruct((t, d), F32)],
        grid=(t // ROW_TILE,),
        in_specs=[row, row, pl.BlockSpec((1, 6, d), _mod_index(n_ctx // ROW_TILE)), vec, vec],
        out_specs=[row, row],
        compiler_params=_cparams(("parallel",)),
        name="ln_res",
    )(xa, a, mods, g.reshape(1, d), b.reshape(1, d))


def _conv_kernel(z_ref, zp_ref, zn_ref, w_ref, o_ref, *, cw, ctx_tiles, n_tiles):
    i = pl.program_id(0)
    starts = jnp.logical_or(i == 0, i == ctx_tiles)
    ends = jnp.logical_or(i == ctx_tiles - 1, i == n_tiles - 1)
    cb = z_ref[:, 0:cw]
    u = z_ref[:, cw:2 * cw] * z_ref[:, 2 * cw:3 * cw]
    up = zp_ref[SUBLANE - 1:SUBLANE, cw:2 * cw] * zp_ref[SUBLANE - 1:SUBLANE, 2 * cw:3 * cw]
    un = zn_ref[0:1, cw:2 * cw] * zn_ref[0:1, 2 * cw:3 * cw]
    up = jnp.where(starts, 0.0, up)
    un = jnp.where(ends, 0.0, un)
    rows = u.shape[0]
    rid = lax.broadcasted_iota(jnp.int32, u.shape, 0)
    u_prev = jnp.where(rid == 0, up, pltpu.roll(u, 1, axis=0))
    u_next = jnp.where(rid == rows - 1, un, pltpu.roll(u, rows - 1, axis=0))
    y = w_ref[0:1, :] * u_prev + w_ref[1:2, :] * u + w_ref[2:3, :] * u_next
    o_ref[...] = (cb * y).astype(o_ref.dtype)


def _short_conv(zc, conv_w, n_ctx):
    t, w3 = zc.shape
    cw = w3 // 3
    n_tiles = t // ROW_TILE
    per = ROW_TILE // SUBLANE
    last8 = t // SUBLANE - 1
    return pl.pallas_call(
        functools.partial(_conv_kernel, cw=cw, ctx_tiles=n_ctx // ROW_TILE, n_tiles=n_tiles),
        out_shape=jax.ShapeDtypeStruct((t, cw), BF16),
        grid=(n_tiles,),
        in_specs=[
            pl.BlockSpec((ROW_TILE, w3), lambda i: (i, 0)),
            pl.BlockSpec((SUBLANE, w3), lambda i: (jnp.maximum(i * per - 1, 0), 0)),
            pl.BlockSpec((SUBLANE, w3), lambda i: (jnp.minimum((i + 1) * per, last8), 0)),
            pl.BlockSpec((CONV_K, cw), lambda i: (0, 0)),
        ],
        out_specs=pl.BlockSpec((ROW_TILE, cw), lambda i: (i, 0)),
        compiler_params=_cparams(("parallel",)),
        name="short_conv",
    )(zc, zc, zc, conv_w)


def _router_kernel(h_ref, w_ref, b_ref, idx_ref, wt_ref, rank_ref, cnt_ref, run_scr):
    @pl.when(pl.program_id(0) == 0)
    def _():
        run_scr[...] = jnp.zeros_like(run_scr)

    logits = jnp.dot(h_ref[...].astype(BF16), w_ref[...], preferred_element_type=F32) + b_ref[...]
    lane = lax.broadcasted_iota(jnp.int32, logits.shape, 1)
    vals, idxs = [], []
    cur = logits
    for _ in range(TOP_K):
        mx = jnp.max(cur, axis=1, keepdims=True)
        ix = jnp.min(jnp.where(cur == mx, lane, LANE), axis=1, keepdims=True)
        vals.append(mx)
        idxs.append(ix)
        cur = jnp.where(lane == ix, -jnp.inf, cur)
    es = [jnp.exp(v - vals[0]) for v in vals]
    tot = es[0]
    for e in es[1:]:
        tot = tot + e
    idx_out = jnp.zeros(logits.shape, jnp.int32)
    wt_out = jnp.zeros(logits.shape, F32)
    for k in range(TOP_K):
        idx_out = jnp.where(lane == k, idxs[k], idx_out)
        wt_out = jnp.where(lane == k, es[k] / tot, wt_out)
    idx_ref[...] = idx_out
    wt_ref[...] = wt_out

    onehots = [(lane == ix).astype(F32) for ix in idxs]
    hist = onehots[0]
    for oh in onehots[1:]:
        hist = hist + oh
    rows = logits.shape[0]
    r_id = lax.broadcasted_iota(jnp.int32, (rows, rows), 0)
    c_id = lax.broadcasted_iota(jnp.int32, (rows, rows), 1)
    strict_lower = (c_id < r_id).astype(BF16)
    before = jnp.dot(strict_lower, hist.astype(BF16), preferred_element_type=F32) + run_scr[...]
    rank_out = jnp.zeros(logits.shape, F32)
    for k in range(TOP_K):
        rank_out = jnp.where(lane == k, jnp.sum(before * onehots[k], axis=1, keepdims=True), rank_out)
    rank_ref[...] = rank_out.astype(jnp.int32)
    run_scr[...] = run_scr[...] + jnp.sum(hist, axis=0, keepdims=True)
    cnt_ref[...] = run_scr[...].astype(jnp.int32)


def _router(hx, w_router_p, b_router_p):
    t, d = hx.shape
    out = pl.BlockSpec((ROW_TILE, LANE), lambda i: (i, 0))
    return pl.pallas_call(
        _router_kernel,
        out_shape=[jax.ShapeDtypeStruct((t, LANE), jnp.int32), jax.ShapeDtypeStruct((t, LANE), F32),
                   jax.ShapeDtypeStruct((t, LANE), jnp.int32), jax.ShapeDtypeStruct((1, LANE), jnp.int32)],
        grid=(t // ROW_TILE,),
        in_specs=[pl.BlockSpec((ROW_TILE, d), lambda i: (i, 0)),
                  pl.BlockSpec((d, LANE), lambda i: (0, 0)),
                  pl.BlockSpec((1, LANE), lambda i: (0, 0))],
        out_specs=[out, out, out, pl.BlockSpec((1, LANE), lambda i: (0, 0))],
        scratch_shapes=[pltpu.VMEM((1, LANE), F32)],
        compiler_params=_cparams(("arbitrary",)),
        name="router",
    )(hx, w_router_p, b_router_p)


def _attn_kernel(q_ref, k_ref, vt_ref, o_ref, s_a, s_b, m_scr, l_scr, acc_scr, *, heads, groups, dk, dv, tk, n_ctx,
                 n_lat_chunks, ctx_tiles):
    i = pl.program_id(1)
    tq = q_ref.shape[0]
    rows = groups * tq
    nt_dims = (((1,), (1,)), ((), ()))
    qs = []
    for j in range(heads):
        c0 = j * groups * dk
        qs.append(jnp.concatenate([q_ref[:, c0 + g * dk:c0 + (g + 1) * dk] for g in range(groups)], axis=0)
                  if groups > 1 else q_ref[:, c0:c0 + dk])

    def scores(j, off, size):
        return lax.dot_general(k_ref[pl.ds(off, size), j * dk:(j + 1) * dk], qs[j], nt_dims,
                               preferred_element_type=F32)

    def store_scores(s_ref, off):
        for j in range(heads):
            s_ref[:, j * rows:(j + 1) * rows] = scores(j, off, tk)

    def accumulate(p, off, size, alpha):
        for j in range(heads):
            cols = slice(j * rows, (j + 1) * rows)
            pv = jnp.dot(vt_ref[j * dv:(j + 1) * dv, pl.ds(off, size)], p[:, cols], preferred_element_type=F32)
            acc_scr[:, cols] = pv if alpha is None else alpha[:, cols] * acc_scr[:, cols] + pv

    def update(s, off, m, l):
        m_new = jnp.maximum(m, jnp.max(s, axis=0, keepdims=True))
        p = jnp.exp2(s - m_new)
        alpha = jnp.exp2(m - m_new)
        l_new = alpha * l + jnp.sum(p, axis=0, keepdims=True)
        accumulate(p.astype(BF16), off, tk, alpha)
        return m_new, l_new

    s0 = [scores(j, 0, n_ctx) for j in range(heads)]
    s0 = s0[0] if heads == 1 else jnp.concatenate(s0, axis=1)
    m0 = jnp.max(s0, axis=0, keepdims=True)
    p0 = jnp.exp2(s0 - m0)
    m_scr[...] = m0
    l_scr[...] = jnp.sum(p0, axis=0, keepdims=True)
    accumulate(p0.astype(BF16), 0, n_ctx, None)

    @pl.when(i >= ctx_tiles)
    def _():
        store_scores(s_a, n_ctx)

        def pair(base, m, l, prefetch):
            store_scores(s_b, base + tk)
            m, l = update(s_a[...], base, m, l)
            if prefetch:
                store_scores(s_a, base + 2 * tk)
            return update(s_b[...], base + tk, m, l)

        def body(c, carry):
            base = pl.multiple_of(n_ctx + c * (2 * tk), LANE)
            return pair(base, carry[0], carry[1], True)

        m, l = lax.fori_loop(0, n_lat_chunks // 2 - 1, body, (m_scr[...], l_scr[...]))
        m, l = pair(n_ctx + (n_lat_chunks - 2) * tk, m, l, False)
        m_scr[...] = m
        l_scr[...] = l

    o_t = acc_scr[...] / l_scr[...]
    for j in range(heads):
        for g in range(groups):
            c0 = j * rows + g * tq
            o0 = (j * groups + g) * dv
            o_ref[:, o0:o0 + dv] = o_t[:, c0:c0 + tq].T.astype(o_ref.dtype)


def _attention(q, k, vt, *, n_kv_heads, heads, groups, dk, dv, tq, tk, n_ctx, k_col0=0):
    t = q.shape[0]
    n_lat = t - n_ctx
    cols = heads * groups * tq
    n_lat_chunks = n_lat // tk
    assert n_lat_chunks % 2 == 0 and n_lat_chunks >= 2 and n_kv_heads % heads == 0 and k_col0 % heads == 0
    resident = dict(pipeline_mode=pl.Buffered(1)) if heads > 1 else {}
    return pl.pallas_call(
        functools.partial(_attn_kernel, heads=heads, groups=groups, dk=dk, dv=dv, tk=tk, n_ctx=n_ctx,
                          n_lat_chunks=n_lat_chunks, ctx_tiles=n_ctx // tq),
        out_shape=jax.ShapeDtypeStruct((t, n_kv_heads * groups * dv), BF16),
        grid=(n_kv_heads // heads, t // tq),
        in_specs=[
            pl.BlockSpec((tq, heads * groups * dk), lambda h, i: (i, h)),
            pl.BlockSpec((t, heads * dk), lambda h, i: (0, k_col0 // heads + h), **resident),
            pl.BlockSpec((heads * dv, t), lambda h, i: (h, 0), **resident),
        ],
        out_specs=pl.BlockSpec((tq, heads * groups * dv), lambda h, i: (i, h)),
        scratch_shapes=[pltpu.VMEM((tk, cols), F32), pltpu.VMEM((tk, cols), F32), pltpu.VMEM((1, cols), F32),
                        pltpu.VMEM((1, cols), F32), pltpu.VMEM((dv, cols), F32)],
        compiler_params=_cparams(("parallel", "parallel")),
        name="prefix_attention",
    )(q, k, vt)


def _branch_kernel(m_ref, g_ref, c_ref, wm_ref, wg_ref, wc_ref, s0_ref, s1_ref, s2_ref, o_ref):
    acc = s0_ref[...] * jnp.dot(m_ref[...], wm_ref[...], preferred_element_type=F32)
    acc = acc + s1_ref[...] * jnp.dot(g_ref[...], wg_ref[...], preferred_element_type=F32)
    acc = acc + s2_ref[...] * jnp.dot(c_ref[...], wc_ref[...], preferred_element_type=F32)
    o_ref[...] = acc.astype(o_ref.dtype)


def _branch_merge(mla, gqa, conv, wm, wg, wc, sig, *, tm, tn):
    t = mla.shape[0]
    d = wm.shape[1]
    nj = d // tn

    def rowspec(a):
        return pl.BlockSpec((tm, a.shape[1]), lambda i, j: (i, 0))

    def wspec(w):
        return pl.BlockSpec((w.shape[0], tn), lambda i, j: (0, j))

    def sigspec(b):
        return pl.BlockSpec((tm, tn), lambda i, j: (i, j + b * nj))

    return pl.pallas_call(
        _branch_kernel,
        out_shape=jax.ShapeDtypeStruct((t, d), BF16),
        grid=(t // tm, nj),
        in_specs=[rowspec(mla), rowspec(gqa), rowspec(conv), wspec(wm), wspec(wg), wspec(wc),
                  sigspec(0), sigspec(1), sigspec(2)],
        out_specs=pl.BlockSpec((tm, tn), lambda i, j: (i, j)),
        compiler_params=_cparams(("parallel", "parallel")),
        name="branch_merge",
    )(mla, gqa, conv, wm, wg, wc, sig, sig, sig)


def _moe_ffn_kernel(te_ref, nt_ref, rt_ref, h_hbm, wgu_ref, bgu_ref, wd_ref, bd_ref, y_ref, buf, sem, *, tm, de):
    t = pl.program_id(0)
    n_used = nt_ref[0]

    def row_copy(tile, slot, r):
        tok = rt_ref[tile * tm + r]
        return pltpu.make_async_copy(h_hbm.at[pl.ds(tok, 1), :], buf.at[slot, pl.ds(r, 1), :], sem.at[slot])

    def start_gather(tile, slot):
        for r in range(tm):
            row_copy(tile, slot, r).start()

    def wait_gather(tile, slot):
        for r in range(tm):
            row_copy(tile, slot, r).wait()

    @pl.when(t == 0)
    def _():
        start_gather(0, 0)

    @pl.when(t < n_used)
    def _():
        slot = t % 2
        wait_gather(t, slot)
        start_gather(jnp.minimum(t + 1, n_used - 1), 1 - slot)
        xs = buf[slot].astype(BF16)
        gu = jnp.dot(xs, wgu_ref[0], preferred_element_type=F32) + bgu_ref[0]
        gate = jnp.minimum(gu[:, :de], SWIGLU_LIMIT)
        up = jnp.clip(gu[:, de:], -SWIGLU_LIMIT, SWIGLU_LIMIT)
        act = (up + 1.0) * (gate * jax.nn.sigmoid(SWIGLU_ALPHA * gate))
        y_ref[...] = jnp.dot(act.astype(BF16), wd_ref[0], preferred_element_type=F32) + bd_ref[0]

    @pl.when(t == n_used - 1)
    def _():
        wait_gather(t, 1 - t % 2)

    @pl.when(t >= n_used)
    def _():
        y_ref[...] = jnp.zeros_like(y_ref)


def _moe_ffn(hx, tile_expert, n_used, row_token, wgu, bgu, wd, bd):
    t, d = hx.shape
    n_exp, _, de2 = wgu.shape
    de = de2 // 2
    n_tiles = tile_expert.shape[0]
    tm = MOE_TILE
    grid_spec = pltpu.PrefetchScalarGridSpec(
        num_scalar_prefetch=3,
        grid=(n_tiles,),
        in_specs=[
            pl.BlockSpec(memory_space=pl.ANY),
            pl.BlockSpec((1, d, de2), lambda i, te, nt, rt: (te[i], 0, 0)),
            pl.BlockSpec((1, 1, de2), lambda i, te, nt, rt: (te[i], 0, 0)),
            pl.BlockSpec((1, de, d), lambda i, te, nt, rt: (te[i], 0, 0)),
            pl.BlockSpec((1, 1, d), lambda i, te, nt, rt: (te[i], 0, 0)),
        ],
        out_specs=pl.BlockSpec((tm, d), lambda i, te, nt, rt: (i, 0)),
        scratch_shapes=[pltpu.VMEM((2, tm, d), F32), pltpu.SemaphoreType.DMA((2,))],
    )
    return pl.pallas_call(
        functools.partial(_moe_ffn_kernel, tm=tm, de=de),
        out_shape=jax.ShapeDtypeStruct((n_tiles * tm, d), F32),
        grid_spec=grid_spec,
        compiler_params=_cparams(("arbitrary",), disable_bounds_checks=True),
        name="moe_ffn",
    )(tile_expert, n_used, row_token, hx, wgu, bgu.reshape(n_exp, 1, de2), wd, bd.reshape(n_exp, 1, d))


def _combine_kernel(pos_ref, y_hbm, wt_ref, x_ref, mod_ref, g_ref, b_ref, *rest, tt, alpha, with_next):
    if with_next:
        nmod_ref, o_ref, h_ref, buf, sem = rest
    else:
        o_ref, buf, sem = rest
    i = pl.program_id(0)
    n = pl.num_programs(0)

    def row_copy(tile, slot, r, k):
        src = pos_ref[(tile * tt + r) * TOP_K + k]
        return pltpu.make_async_copy(y_hbm.at[pl.ds(src, 1), :], buf.at[slot, k, pl.ds(r, 1), :], sem.at[slot])

    def start_gather(tile, slot):
        for r in range(tt):
            for k in range(TOP_K):
                row_copy(tile, slot, r, k).start()

    def wait_gather(tile, slot):
        for r in range(tt):
            for k in range(TOP_K):
                row_copy(tile, slot, r, k).wait()

    @pl.when(i == 0)
    def _():
        start_gather(0, 0)

    slot = i % 2
    wait_gather(i, slot)
    start_gather(jnp.minimum(i + 1, n - 1), 1 - slot)
    wt = wt_ref[...]
    f = wt[:, 0:1] * buf[slot, 0]
    for k in range(1, TOP_K):
        f = f + wt[:, k:k + 1] * buf[slot, k]
    m = mod_ref[0]
    y = _layernorm_rows(alpha * x_ref[...] + m[5:6, :] * f, g_ref[...], b_ref[...])
    o_ref[...] = y
    if with_next:
        nm = nmod_ref[0]
        h_ref[...] = (y * (1.0 + nm[1:2, :]) + nm[0:1, :]).astype(h_ref.dtype)

    @pl.when(i == n - 1)
    def _():
        wait_gather(i, 1 - slot)


def _moe_combine(pos, y_sorted, wts, xa, mods, g, b, next_mods, n_ctx, alpha):
    t, d = xa.shape
    tt = COMBINE_TILE
    with_next = next_mods is not None
    ctx_tiles = n_ctx // tt
    modspec = pl.BlockSpec((1, 6, d), lambda i, pos: (jnp.where(i < ctx_tiles, 1, 0), 0, 0))
    row = pl.BlockSpec((tt, d), lambda i, pos: (i, 0))
    vec = pl.BlockSpec((1, d), lambda i, pos: (0, 0))
    in_specs = [pl.BlockSpec(memory_space=pl.ANY), pl.BlockSpec((tt, LANE), lambda i, pos: (i, 0)), row, modspec,
                vec, vec]
    args = [pos, y_sorted, wts, xa, mods, g.reshape(1, d), b.reshape(1, d)]
    out_shape = [jax.ShapeDtypeStruct((t, d), F32)]
    out_specs = [row]
    if with_next:
        in_specs.append(modspec)
        args.append(next_mods)
        out_shape.append(jax.ShapeDtypeStruct((t, d), BF16))
        out_specs.append(row)
    grid_spec = pltpu.PrefetchScalarGridSpec(
        num_scalar_prefetch=1,
        grid=(t // tt,),
        in_specs=in_specs,
        out_specs=out_specs,
        scratch_shapes=[pltpu.VMEM((2, TOP_K, tt, d), F32), pltpu.SemaphoreType.DMA((2,))],
    )
    res = pl.pallas_call(
        functools.partial(_combine_kernel, tt=tt, alpha=alpha, with_next=with_next),
        out_shape=out_shape,
        grid_spec=grid_spec,
        compiler_params=_cparams(("arbitrary",), disable_bounds_checks=True),
        name="moe_combine_ln",
    )(*args)
    return res if with_next else (res[0], None)


def _permcast_kernel(w_ref, p_ref, o_ref):
    o_ref[0] = jnp.dot(w_ref[0, 0].astype(BF16), p_ref[...], preferred_element_type=F32).astype(o_ref.dtype)


def _deinterleave_cols(w, li):
    _, e, k, n = w.shape
    src_col = jnp.concatenate([jnp.arange(0, n, 2), jnp.arange(1, n, 2)])
    perm = (jnp.arange(n)[:, None] == src_col[None, :]).astype(BF16)
    tk = _div_tile(k, 1024, 2 * SUBLANE)
    return pl.pallas_call(
        _permcast_kernel,
        out_shape=jax.ShapeDtypeStruct((e, k, n), BF16),
        grid=(e, k // tk),
        in_specs=[pl.BlockSpec((1, 1, tk, n), lambda ei, i: (li, ei, i, 0)),
                  pl.BlockSpec((n, n), lambda ei, i: (0, 0))],
        out_specs=pl.BlockSpec((1, tk, n), lambda ei, i: (ei, i, 0)),
        compiler_params=_cparams(("parallel", "parallel")),
        name="deinterleave_cols",
    )(w, perm)


def _routing_tables(top_idx, rank, sizes, tm):
    t = top_idx.shape[0]
    n_exp = sizes.shape[0]
    n_pairs = t * TOP_K
    n_tiles = -(-(n_pairs + n_exp * (tm - 1)) // tm)
    padded = ((sizes + tm - 1) // tm) * tm
    pend = jnp.cumsum(padded)
    pstart = pend - padded
    pos = (pstart[top_idx] + rank).reshape(-1)
    row_token = jnp.zeros((n_tiles * tm,), jnp.int32).at[pos].set(jnp.arange(n_pairs, dtype=jnp.int32) // TOP_K)
    tile_start = jnp.arange(n_tiles, dtype=jnp.int32) * tm
    tile_expert = jnp.minimum(jnp.searchsorted(pend, tile_start, side="right"), n_exp - 1).astype(jnp.int32)
    n_used = (pend[-1] // tm).astype(jnp.int32).reshape(1)
    return tile_expert, n_used, row_token, pos


def _rope_tables(n_ctx, n_lat, rot_dim):
    rows = n_lat // GRID_W
    row = jnp.repeat(jnp.arange(rows, dtype=F32), GRID_W)
    col = jnp.tile(jnp.arange(GRID_W, dtype=F32), rows)
    nf = rot_dim // 4
    inv = ROPE_BASE ** (-jnp.arange(nf, dtype=F32) / nf)
    ang = jnp.concatenate([row[:, None] * inv, col[:, None] * inv], axis=-1)
    ang = jnp.concatenate([jnp.zeros((n_ctx, rot_dim // 2), F32), ang], axis=0)
    return jnp.cos(ang), jnp.sin(ang)


def _pad_half(a, width):
    return jnp.pad(a, ((0, 0), (0, width - a.shape[1])))


def kernel(x, c, ctx, c_ctx, w_ada, b_ada, w_in, g_q_lat, g_kv_lat, w_q_up, w_kv_up, g_q_head, g_k_head, conv_w,
           w_br_mla, w_br_gqa, w_br_conv, w_out, ln1_g, ln1_b, w_router, b_router, w_gate_up, b_gate_up, w_down,
           b_down, ln2_g, ln2_b):
    bsz, s, d = x.shape
    assert bsz == 1, "single-sample prefill"
    n_ctx = ctx.shape[1]
    t = n_ctx + s
    depth = w_ada.shape[0]
    ql = g_q_lat.shape[1]
    kvl = g_kv_lat.shape[1]
    cw = conv_w.shape[2]
    n_exp = w_router.shape[2]
    de = w_down.shape[2]
    alpha = (2 * depth) ** 0.25
    assert n_ctx % ROW_TILE == 0 and s % ROW_TILE == 0 and n_exp <= LANE
    half = MLA_ROPE // 2

    cos_a, sin_a = _rope_tables(n_ctx, s, MLA_ROPE)
    cos_a_p = jnp.concatenate([_pad_half(cos_a, LANE // 2), _pad_half(cos_a, LANE // 2)], axis=1)
    sin_a_p = jnp.concatenate([_pad_half(-sin_a, LANE // 2), _pad_half(sin_a, LANE // 2)], axis=1)
    cos_b, sin_b = _rope_tables(n_ctx, s, GQA_HD)
    cos_b_p = jnp.concatenate([cos_b, cos_b], axis=1)
    sin_b_p = jnp.concatenate([-sin_b, sin_b], axis=1)

    cc = jnp.zeros((SUBLANE, d), F32).at[0].set(c[0]).at[1].set(c_ctx)
    mods_all = _ada(cc, w_ada, b_ada).reshape(depth, SUBLANE, 6, d)[:, :2]

    xa = jnp.concatenate([ctx[0], x[0]], axis=0)
    h = _modulate(xa, mods_all[0], n_ctx)

    tm_big = _div_tile(t, 1280, ROW_TILE)
    tm_mid = _div_tile(t, 640, SUBLANE * 16)
    q_scale_mla = (MLA_NOPE + MLA_ROPE) ** -0.5 * LOG2E
    q_scale_gqa = GQA_HD ** -0.5 * LOG2E

    for li in range(depth):
        mods = mods_all[li]
        wi = w_in[li]
        o = 0
        w_qlat = wi[:, o:o + ql].astype(BF16); o += ql
        w_kvlat = wi[:, o:o + kvl]; o += kvl
        w_kpe = wi[:, o:o + MLA_ROPE]; o += MLA_ROPE
        w_kpe_p = jnp.concatenate([_pad_half(w_kpe[:, :half], LANE // 2), _pad_half(w_kpe[:, half:], LANE // 2)], 1)
        w_kvpe = jnp.concatenate([w_kvlat, w_kpe_p], axis=1).astype(BF16)
        n_gqk = (GQA_Q_HEADS + GQA_KV_HEADS) * GQA_HD
        w_gqk = wi[:, o:o + n_gqk].astype(BF16); o += n_gqk
        n_gv = GQA_KV_HEADS * GQA_HD
        w_gv = wi[:, o:o + n_gv].astype(BF16); o += n_gv
        w_cv = wi[:, o:o + 3 * cw].astype(BF16); o += 3 * cw
        w_gt = wi[:, o:o + N_BRANCH * d].astype(BF16); o += N_BRANCH * d
        assert o == wi.shape[1]

        wq = w_q_up[li].reshape(ql, MLA_HEADS, MLA_NOPE + MLA_ROPE)
        zq = jnp.zeros((ql, MLA_HEADS, LANE // 2 - half), wq.dtype)
        wq_p = jnp.concatenate([wq[:, :, :MLA_NOPE], wq[:, :, MLA_NOPE:MLA_NOPE + half], zq,
                                wq[:, :, MLA_NOPE + half:], zq], axis=2).reshape(ql, MLA_HEADS * 2 * LANE)
        wq_p = wq_p.astype(BF16)
        wkv = w_kv_up[li].reshape(kvl, MLA_HEADS, MLA_NOPE + MLA_V)
        wkv_p = jnp.concatenate([wkv[:, :, :MLA_NOPE].reshape(kvl, -1), wkv[:, :, MLA_NOPE:].reshape(kvl, -1)],
                                axis=1).astype(BF16)

        gain_qk = jnp.concatenate([jnp.tile(g_q_head[li], GQA_Q_HEADS), jnp.tile(g_k_head[li], GQA_KV_HEADS)])
        scale_qk = jnp.concatenate([jnp.full((GQA_Q_HEADS * GQA_HD,), q_scale_gqa, F32),
                                    jnp.ones((GQA_KV_HEADS * GQA_HD,), F32)])
        g_kvpe = jnp.concatenate([g_kv_lat[li], jnp.ones((LANE,), F32)]).reshape(1, kvl + LANE)

        (qn,) = _matmul(h, w_qlat, tm=tm_mid, tn=ql, outs=[(ql, BF16)], epilogue=_ep_rms,
                        cols=[g_q_lat[li].reshape(1, ql)], name="in_qlat")
        kvn, kpe = _matmul(h, w_kvpe, tm=tm_mid, tn=kvl + LANE, outs=[(kvl, BF16), (LANE, BF16)],
                           epilogue=functools.partial(_ep_kvpe, kvl=kvl), rows=[cos_a_p, sin_a_p], cols=[g_kvpe],
                           name="in_kvpe")
        (gqk,) = _matmul(h, w_gqk, tm=tm_big, tn=4 * GQA_HD, outs=[(n_gqk, BF16)], epilogue=_ep_gqk,
                         rows=[cos_b_p, sin_b_p], cols=[gain_qk.reshape(1, -1), scale_qk.reshape(1, -1)],
                         name="in_gqk")
        (gv,) = _matmul(h, w_gv, tm=tm_big, tn=n_gv, outs=[(n_gv, BF16)], epilogue=_ep_store, name="in_gv")
        (zc,) = _matmul(h, w_cv, tm=tm_big, tn=_div_tile(3 * cw, 512, LANE), outs=[(3 * cw, F32)],
                        epilogue=_ep_store, name="in_conv")
        (sig,) = _matmul(h, w_gt, tm=tm_big, tn=_div_tile(N_BRANCH * d, 512, LANE), outs=[(N_BRANCH * d, F32)],
                         epilogue=_ep_sigmoid, name="in_gates")

        (q_mla,) = _matmul(qn, wq_p, tm=tm_big, tn=MLA_HEADS * 2 * LANE, outs=[(MLA_HEADS * 2 * LANE, BF16)],
                           epilogue=functools.partial(_ep_qup, qscale=q_scale_mla), rows=[cos_a_p, sin_a_p],
                           name="mla_q_up")
        k_mla, v_mla = _matmul(kvn, wkv_p, tm=tm_big, tn=MLA_HEADS * 2 * LANE,
                               outs=[(MLA_HEADS * 2 * LANE, BF16), (MLA_HEADS * MLA_V, BF16)], epilogue=_ep_kvup,
                               rows=[kpe], name="mla_kv_up")
        mla = _attention(q_mla, k_mla, v_mla.T, n_kv_heads=MLA_HEADS, heads=2, groups=1, dk=2 * LANE, dv=MLA_V, tq=ROW_TILE,
                         tk=_div_tile(s // 2, MLA_KEY_CHUNK, ROW_TILE), n_ctx=n_ctx)

        gqa = _attention(gqk, gqk, gv.T, n_kv_heads=GQA_KV_HEADS, heads=1, groups=GQA_Q_HEADS // GQA_KV_HEADS, dk=GQA_HD,
                         dv=GQA_HD, tq=ROW_TILE // 2, tk=_div_tile(s // 2, GQA_KEY_CHUNK, ROW_TILE), n_ctx=n_ctx,
                         k_col0=GQA_Q_HEADS)

        conv = _short_conv(zc, conv_w[li], n_ctx)

        merged = _branch_merge(mla, gqa, conv, w_br_mla[li].astype(BF16), w_br_gqa[li].astype(BF16),
                               w_br_conv[li].astype(BF16), sig, tm=tm_mid, tn=_div_tile(d, 512, LANE))
        (a,) = _matmul(merged, w_out[li].astype(BF16), tm=tm_big, tn=_div_tile(d, 512, LANE), outs=[(d, F32)],
                       epilogue=_ep_store, name="w_out")
        xa, hx = _ln_res(xa, a, mods, ln1_g[li], ln1_b[li], n_ctx, alpha)

        w_r = _pad_half(w_router[li], LANE).astype(BF16)
        b_r = jnp.concatenate([b_router[li], jnp.full((LANE - n_exp,), NEG_BIG, F32)]).reshape(1, LANE)
        top_idx, top_w, top_rank, counts = _router(hx, w_r, b_r)
        tile_expert, n_used, row_token, pos = _routing_tables(top_idx[:, :TOP_K], top_rank[:, :TOP_K],
                                                              counts[0, :n_exp], MOE_TILE)
        wgu_p = _deinterleave_cols(w_gate_up, li)
        bgu = b_gate_up[li]
        bgu_p = jnp.concatenate([bgu[:, 0::2], bgu[:, 1::2]], axis=1)
        y_sorted = _moe_ffn(hx, tile_expert, n_used, row_token, wgu_p, bgu_p, w_down[li].astype(BF16), b_down[li])
        next_mods = mods_all[li + 1] if li + 1 < depth else None
        xa, h = _moe_combine(pos, y_sorted, top_w, xa, mods, ln2_g[li], ln2_b[li], next_mods, n_ctx, alpha)

    return xa[n_ctx:].reshape(bsz, s, d)
```
